```python
import math
import jax
import jax.numpy as jnp
from jax import lax
import numpy as np


D_MODEL = 1024
BATCH = 2
SEQ = 8192
DEPTH = 2

F32 = jnp.float32
HEAD_DIM = 64
BLOCK = 128
ROPE_THETA = 500000.0
ROT_DIM = HEAD_DIM // 4
EPS = 1e-6
MASK_VALUE = -1e30
LB_FLOOR = 1e-30
N_BRANCH = 4

HGRN_HEADS = 4
HGRN_DIM = 128
HGRN_WIDTH = HGRN_HEADS * HGRN_DIM
HGRN_CHUNK = 64

SWA_Q_HEADS = 8
SWA_KV_HEADS = 2
SWA_WINDOW = 128

S5_WIDTH = 512
S5_GROUP = 16
S5_GROUPS = S5_WIDTH // S5_GROUP
S5_STATE = 64

DIL_PAIRS = ((128, 1), (512, 4), (2048, 16))
DIL_HEADS_PER_GROUP = 4
DIL_HEADS = DIL_HEADS_PER_GROUP * len(DIL_PAIRS)
DIL_OUT = DIL_HEADS_PER_GROUP * HEAD_DIM

D_FF = 2816
CONV_WIDTH = 3

IN_SIZES = (HGRN_WIDTH,) * 4 + (SWA_Q_HEADS * HEAD_DIM, SWA_KV_HEADS * HEAD_DIM, SWA_KV_HEADS * HEAD_DIM) + (S5_WIDTH,) + (DIL_HEADS * HEAD_DIM,) * 3 + (N_BRANCH * D_MODEL,)
D_IN = sum(IN_SIZES)

kernel_name = 'hybrid_gated_parallel_mixer_trunk'


def rms_norm(x, g):
    xf = x.astype(F32)
    y = xf * lax.rsqrt(jnp.mean(xf * xf, axis=-1, keepdims=True) + EPS)
    return (y * g.astype(F32)).astype(x.dtype)


def split_columns(proj):
    points = np.cumsum(np.array(IN_SIZES))[:-1].tolist()
    return jnp.split(proj, points, axis=-1)


def rope_tables(seq):
    inv_freq = ROPE_THETA ** (-jnp.arange(0, ROT_DIM, 2, dtype=F32) / ROT_DIM)
    ang = jnp.arange(seq, dtype=F32)[:, None] * inv_freq[None, :]
    return jnp.cos(ang), jnp.sin(ang)


def apply_partial_rope(t, cos, sin):
    half = ROT_DIM // 2
    tr = t[..., :ROT_DIM].astype(F32)
    t1, t2 = tr[..., :half], tr[..., half:]
    c, s = cos[None, :, None, :], sin[None, :, None, :]
    rot = jnp.concatenate([t1 * c - t2 * s, t2 * c + t1 * s], axis=-1).astype(t.dtype)
    return jnp.concatenate([rot, t[..., ROT_DIM:]], axis=-1)


def banded_attention(q, k, v, max_dist, sinks=None):
    bsz, length, hq, hd = q.shape
    hkv = k.shape[2]
    rep = hq // hkv
    nb = -(-length // BLOCK)
    pad = nb * BLOCK - length

    def pad_len(t):
        return jnp.pad(t, ((0, 0), (0, pad), (0, 0), (0, 0)))

    qb = pad_len(q).reshape(bsz, nb, BLOCK, hkv, rep, hd)
    kb = pad_len(k).reshape(bsz, nb, BLOCK, hkv, hd)
    vb = pad_len(v).reshape(bsz, nb, BLOCK, hkv, hd)

    def with_prev(t):
        prev = jnp.pad(t, ((0, 0), (1, 0), (0, 0), (0, 0), (0, 0)))[:, :-1]
        return jnp.concatenate([prev, t], axis=2)

    kw, vw = with_prev(kb), with_prev(vb)
    s = jnp.einsum('bnqgrd,bnkgd->bngrqk', qb, kw, preferred_element_type=F32) * (hd ** -0.5)
    qpos = jnp.arange(BLOCK) + BLOCK
    kpos = jnp.arange(2 * BLOCK)
    dist = qpos[:, None] - kpos[None, :]
    band = (dist >= 0) & (dist <= max_dist)
    valid = band[None] & ((jnp.arange(nb) > 0)[:, None, None] | (kpos >= BLOCK)[None, None, :])
    s = jnp.where(valid[None, :, None, None], s, MASK_VALUE)
    m = jnp.max(s, axis=-1)
    if sinks is not None:
        sink = sinks.astype(F32).reshape(hkv, rep)[None, None, :, :, None]
        m = jnp.maximum(m, sink)
    p = jnp.exp(s - m[..., None])
    denom = jnp.sum(p, axis=-1)
    if sinks is not None:
        denom = denom + jnp.exp(sink - m)
    o = jnp.einsum('bngrqk,bnkgd->bnqgrd', p, vw.astype(F32))
    o = o / jnp.moveaxis(denom, -1, 2)[..., None]
    lse = jnp.moveaxis(m + jnp.log(denom), -1, 2)
    o = o.reshape(bsz, nb * BLOCK, hq, hd)[:, :length].astype(q.dtype)
    lse = lse.reshape(bsz, nb * BLOCK, hq)[:, :length]
    return o, lse


def dilated_attention(q, k, v, window, dilation):
    bsz, seq, h, hd = q.shape
    length = seq // dilation

    def to_sub(t):
        return t.reshape(bsz, length, dilation, h, hd).transpose(0, 2, 1, 3, 4).reshape(bsz * dilation, length, h, hd)

    o, lse = banded_attention(to_sub(q), to_sub(k), to_sub(v), window // dilation)
    o = o.reshape(bsz, dilation, length, h, hd).transpose(0, 2, 1, 3, 4).reshape(bsz, seq, h, hd)
    lse = lse.reshape(bsz, dilation, length, h).transpose(0, 2, 1, 3).reshape(bsz, seq, h)
    return o, lse


def chunk_gated_recurrence(q, k, v, log_f):
    bsz, seq, h, dk = q.shape
    dv = v.shape[-1]
    nc = seq // HGRN_CHUNK

    def to_chunks(t):
        return t.reshape(bsz, nc, HGRN_CHUNK, h, t.shape[-1]).transpose(1, 0, 3, 2, 4)

    causal = jnp.tril(jnp.ones((HGRN_CHUNK, HGRN_CHUNK), dtype=bool))

    def step(state, inp):
        qc, kc, vc, gc = inp
        b = jnp.cumsum(gc, axis=2)
        diff = jnp.where(causal[None, None, :, :, None], b[:, :, :, None, :] - b[:, :, None, :, :], MASK_VALUE)
        scores = jnp.einsum('bhtk,bhsk,bhtsk->bhts', qc, kc, jnp.exp(diff))
        o = jnp.einsum('bhts,bhsv->bhtv', scores, vc) + jnp.einsum('bhtk,bhkv->bhtv', qc * jnp.exp(b), state)
        b_last = b[:, :, -1:, :]
        state = jnp.exp(b_last[:, :, 0, :])[..., None] * state + jnp.einsum('bhsk,bhsv->bhkv', kc * jnp.exp(b_last - b), vc)
        return state, o

    init = jnp.zeros((bsz, h, dk, dv), F32)
    _, o = lax.scan(step, init, (to_chunks(q), to_chunks(k), to_chunks(v), to_chunks(log_f)))
    return o.transpose(1, 0, 3, 2, 4).reshape(bsz, seq, h, dv)


def hgrn2_mixer(q, f_logit, i, g, lower_bound, norm_g):
    bsz, seq, _ = q.shape
    shp = (bsz, seq, HGRN_HEADS, HGRN_DIM)
    qf = jax.nn.silu(q.astype(F32)).reshape(shp)
    xf = f_logit.astype(F32).reshape(shp)
    lb = lower_bound.astype(F32).reshape(HGRN_HEADS, HGRN_DIM)
    log_f = jnp.logaddexp(jnp.log(jnp.maximum(lb, LB_FLOOR)), jnp.log1p(-lb) + jax.nn.log_sigmoid(xf))
    kf = (1.0 - lb) * jax.nn.sigmoid(-xf)
    vf = i.astype(F32).reshape(shp)
    o = chunk_gated_recurrence(qf, kf, vf, log_f)
    o = rms_norm(o, norm_g) * jax.nn.silu(g.astype(F32).reshape(shp))
    return o.reshape(bsz, seq, HGRN_WIDTH)


def swa_sink_mixer(q, k, v, sinks, cos, sin):
    bsz, seq, _ = q.shape
    q = apply_partial_rope(q.reshape(bsz, seq, SWA_Q_HEADS, HEAD_DIM), cos, sin)
    k = apply_partial_rope(k.reshape(bsz, seq, SWA_KV_HEADS, HEAD_DIM), cos, sin)
    v = v.reshape(bsz, seq, SWA_KV_HEADS, HEAD_DIM)
    o, _ = banded_attention(q, k, v, SWA_WINDOW - 1, sinks)
    return o.reshape(bsz, seq, SWA_Q_HEADS * HEAD_DIM)


def _diag_recurrence_op(left, right):
    a_l, b_l = left
    a_r, b_r = right
    return a_r * a_l, a_r * b_l + b_r


def s5_mixer(u, a_re, a_im, b_re, b_im, c_re, c_im, d, log_step, glu_w, glu_b):
    bsz, seq, _ = u.shape
    uf = u.astype(F32).reshape(bsz, seq, S5_GROUPS, S5_GROUP)
    a = lax.complex(a_re.astype(F32), a_im.astype(F32))
    dt = jnp.exp(log_step.astype(F32))[:, None]
    a_bar = jnp.exp(dt * a)
    b = lax.complex(b_re.astype(F32), b_im.astype(F32))
    b_bar = ((a_bar - 1.0) / a)[..., None] * b
    c = lax.complex(c_re.astype(F32), c_im.astype(F32))
    bu = jnp.einsum('bsgi,gpi->bsgp', uf.astype(jnp.complex64), b_bar)
    a_seq = jnp.broadcast_to(a_bar, bu.shape)
    _, states = lax.associative_scan(_diag_recurrence_op, (a_seq, bu), axis=1)
    y = jnp.real(jnp.einsum('gip,bsgp->bsgi', c, states)) + d.astype(F32).reshape(S5_GROUPS, S5_GROUP) * uf
    y = jax.nn.gelu(y.reshape(bsz, seq, S5_WIDTH))
    return y * jax.nn.sigmoid(jnp.einsum('bse,ef->bsf', y, glu_w.astype(F32)) + glu_b.astype(F32))


def dilated_mixer(q, k, v, cos, sin):
    bsz, seq, _ = q.shape
    shp = (bsz, seq, DIL_HEADS, HEAD_DIM)
    q = apply_partial_rope(q.reshape(shp), cos, sin)
    k = apply_partial_rope(k.reshape(shp), cos, sin)
    v = v.reshape(shp)
    outs, lses = [], []
    for g, (window, dilation) in enumerate(DIL_PAIRS):
        hs = slice(g * DIL_HEADS_PER_GROUP, (g + 1) * DIL_HEADS_PER_GROUP)
        o, lse = dilated_attention(q[:, :, hs], k[:, :, hs], v[:, :, hs], window, dilation)
        outs.append(o)
        lses.append(lse)
    w = jax.nn.softmax(jnp.stack(lses), axis=0)
    o = jnp.einsum('gbsh,gbshd->bshd', w, jnp.stack(outs).astype(F32))
    return o.reshape(bsz, seq, DIL_OUT)


def mixer_block(h, w_in, lower_bound, hgrn_norm_g, attn_sinks, s5_a_re, s5_a_im, s5_b_re, s5_b_im, s5_c_re, s5_c_im, s5_d, s5_log_step, s5_glu_w, s5_glu_b, w_branch_a, w_branch_b, w_branch_c, w_branch_d, w_out, cos, sin):
    bsz, seq, _ = h.shape
    proj = jnp.einsum('bsd,de->bse', h, w_in)
    (a_q, a_f, a_i, a_g, b_q, b_k, b_v, c_u, d_q, d_k, d_v, gate_logits) = split_columns(proj)
    y_a = hgrn2_mixer(a_q, a_f, a_i, a_g, lower_bound, hgrn_norm_g)
    y_b = swa_sink_mixer(b_q, b_k, b_v, attn_sinks, cos, sin)
    y_c = s5_mixer(c_u, s5_a_re, s5_a_im, s5_b_re, s5_b_im, s5_c_re, s5_c_im, s5_d, s5_log_step, s5_glu_w, s5_glu_b)
    y_d = dilated_mixer(d_q, d_k, d_v, cos, sin)
    gates = jax.nn.sigmoid(gate_logits.astype(F32)).reshape(bsz, seq, N_BRANCH, D_MODEL)

    def branch(y, w):
        return jnp.einsum('bse,ed->bsd', y.astype(h.dtype), w).astype(F32)

    merged = (gates[:, :, 0] * branch(y_a, w_branch_a) + gates[:, :, 1] * branch(y_b, w_branch_b)
              + gates[:, :, 2] * branch(y_c, w_branch_c) + gates[:, :, 3] * branch(y_d, w_branch_d))
    return jnp.einsum('bsd,de->bse', merged.astype(h.dtype), w_out)


def conv_ffn(h, w_up, conv_w, conv_b, w_down):
    u = jnp.einsum('bsd,df->bsf', h, w_up)
    u = lax.conv_general_dilated(u, conv_w[:, None, :], window_strides=(1,), padding=[(CONV_WIDTH - 1, 0)],
                                 dimension_numbers=('NWC', 'WIO', 'NWC'), feature_group_count=2 * D_FF) + conv_b
    a, b = jnp.split(u, 2, axis=-1)
    return jnp.einsum('bsf,fd->bsd', jax.nn.silu(a) * b, w_down)


def setup_inputs(seed: int = 0) -> dict:
    key = jax.random.key(seed)
    ks = jax.random.split(key, 27)
    L = DEPTH

    def nrm(k, shape, scale):
        return jax.random.normal(k, shape, F32) * scale

    def gain(k, shape):
        return 1.0 + 0.01 * jax.random.normal(k, shape, F32)

    n_idx = jnp.arange(S5_STATE, dtype=F32)
    return {
        'x': nrm(ks[0], (BATCH, SEQ, D_MODEL), 1.0),
        'norm1_g': gain(ks[1], (L, D_MODEL)),
        'w_in': nrm(ks[2], (L, D_MODEL, D_IN), D_MODEL ** -0.5),
        'hgrn_lb_logits': nrm(ks[3], (L, HGRN_WIDTH), 0.1),
        'hgrn_norm_g': gain(ks[4], (L, HGRN_DIM)),
        'attn_sinks': nrm(ks[5], (L, SWA_Q_HEADS), 0.5),
        's5_a_re': -0.5 + nrm(ks[6], (L, S5_GROUPS, S5_STATE), 0.01),
        's5_a_im': math.pi * n_idx + nrm(ks[7], (L, S5_GROUPS, S5_STATE), 0.01),
        's5_b_re': nrm(ks[8], (L, S5_GROUPS, S5_STATE, S5_GROUP), (2 * S5_GROUP) ** -0.5),
        's5_b_im': nrm(ks[9], (L, S5_GROUPS, S5_STATE, S5_GROUP), (2 * S5_GROUP) ** -0.5),
        's5_c_re': nrm(ks[10], (L, S5_GROUPS, S5_GROUP, S5_STATE), (2 * S5_STATE) ** -0.5),
        's5_c_im': nrm(ks[11], (L, S5_GROUPS, S5_GROUP, S5_STATE), (2 * S5_STATE) ** -0.5),
        's5_d': nrm(ks[12], (L, S5_WIDTH), 1.0),
        's5_log_step': jax.random.uniform(ks[13], (L, S5_GROUPS), F32, math.log(1e-3), math.log(1e-1)),
        's5_glu_w': nrm(ks[14], (L, S5_WIDTH, S5_WIDTH), S5_WIDTH ** -0.5),
        's5_glu_b': nrm(ks[15], (L, S5_WIDTH), 0.01),
        'w_branch_a': nrm(ks[16], (L, HGRN_WIDTH, D_MODEL), HGRN_WIDTH ** -0.5),
        'w_branch_b': nrm(ks[17], (L, SWA_Q_HEADS * HEAD_DIM, D_MODEL), (SWA_Q_HEADS * HEAD_DIM) ** -0.5),
        'w_branch_c': nrm(ks[18], (L, S5_WIDTH, D_MODEL), S5_WIDTH ** -0.5),
        'w_branch_d': nrm(ks[19], (L, DIL_OUT, D_MODEL), DIL_OUT ** -0.5),
        'w_out': nrm(ks[20], (L, D_MODEL, D_MODEL), D_MODEL ** -0.5),
        'norm2_g': gain(ks[21], (L, D_MODEL)),
        'ffn_w_up': nrm(ks[22], (L, D_MODEL, 2 * D_FF), D_MODEL ** -0.5),
        'ffn_conv_w': nrm(ks[23], (L, CONV_WIDTH, 2 * D_FF), CONV_WIDTH ** -0.5),
        'ffn_conv_b': nrm(ks[24], (L, 2 * D_FF), 0.01),
        'ffn_w_down': nrm(ks[25], (L, D_FF, D_MODEL), D_FF ** -0.5),
        'final_norm_g': gain(ks[26], (D_MODEL,)),
    }


def reference(x, norm1_g, w_in, hgrn_lb_logits, hgrn_norm_g, attn_sinks, s5_a_re, s5_a_im, s5_b_re, s5_b_im, s5_c_re, s5_c_im, s5_d, s5_log_step, s5_glu_w, s5_glu_b, w_branch_a, w_branch_b, w_branch_c, w_branch_d, w_out, norm2_g, ffn_w_up, ffn_conv_w, ffn_conv_b, ffn_w_down, final_norm_g):
    cos, sin = rope_tables(x.shape[1])
    p = jax.nn.softmax(hgrn_lb_logits.astype(F32), axis=0)
    lower_bounds = jnp.cumsum(p, axis=0) - p[0]
    h = x
    for l in range(DEPTH):
        hn = rms_norm(h, norm1_g[l])
        h = h + mixer_block(hn, w_in[l], lower_bounds[l], hgrn_norm_g[l], attn_sinks[l], s5_a_re[l], s5_a_im[l],
                            s5_b_re[l], s5_b_im[l], s5_c_re[l], s5_c_im[l], s5_d[l], s5_log_step[l], s5_glu_w[l],
                            s5_glu_b[l], w_branch_a[l], w_branch_b[l], w_branch_c[l], w_branch_d[l], w_out[l], cos, sin)
        h = h + conv_ffn(rms_norm(h, norm2_g[l]), ffn_w_up[l], ffn_conv_w[l], ffn_conv_b[l], ffn_w_down[l])
    return rms_norm(h, final_norm_g)
```

```python
import functools
import math

import numpy as np
import jax
import jax.numpy as jnp
from jax import lax
from jax.experimental import pallas as pl
from jax.experimental.pallas import tpu as pltpu

F32 = jnp.float32
BF16 = jnp.bfloat16

D_MODEL = 1024
DEPTH = 2
HEAD_DIM = 64
BLOCK = 128
ROPE_THETA = 500000.0
ROT_DIM = HEAD_DIM // 4
EPS = 1e-6
MASK_VALUE = -1e30
LB_FLOOR = 1e-30
N_BRANCH = 4

HGRN_HEADS = 4
HGRN_DIM = 128
HGRN_WIDTH = HGRN_HEADS * HGRN_DIM
HGRN_CHUNK = 128
HGRN_SUB = 16

SWA_Q_HEADS = 8
SWA_KV_HEADS = 2
SWA_WINDOW = 128
SWA_Q_WIDTH = SWA_Q_HEADS * HEAD_DIM
SWA_KV_WIDTH = SWA_KV_HEADS * HEAD_DIM

S5_WIDTH = 512
S5_GROUP = 16
S5_GROUPS = S5_WIDTH // S5_GROUP
S5_STATE = 64
S5_COLS = 4
S5_TILE = 512
S5_LANES = 8

DIL_PAIRS = ((128, 1), (512, 4), (2048, 16))
DIL_HEADS_PER_GROUP = 4
DIL_HEADS = DIL_HEADS_PER_GROUP * len(DIL_PAIRS)
DIL_OUT = DIL_HEADS_PER_GROUP * HEAD_DIM
DIL_NQ = (8, 2, 1)

D_FF = 2816
CONV_WIDTH = 3
FFN_TF = 256
FFN_TM = 512
FFN_HALO = 16

LANES = 128

OFF_GATE = 0
OFF_HGRN = 4096
OFF_SWA_Q = OFF_HGRN + 4 * HGRN_WIDTH
OFF_S5 = OFF_SWA_Q + SWA_Q_WIDTH
OFF_DIL_Q = OFF_S5 + S5_WIDTH
OFF_DIL_K = OFF_DIL_Q + DIL_HEADS * HEAD_DIM
OFF_DIL_V = OFF_DIL_K + DIL_HEADS * HEAD_DIM
OFF_SWA_K = OFF_DIL_V + DIL_HEADS * HEAD_DIM
OFF_SWA_V = OFF_SWA_K + SWA_KV_WIDTH
D_IN = OFF_SWA_V + SWA_KV_WIDTH

_REF_SIZES = (HGRN_WIDTH,) * 4 + (SWA_Q_WIDTH, SWA_KV_WIDTH, SWA_KV_WIDTH, S5_WIDTH) + (DIL_HEADS * HEAD_DIM,) * 3 + (N_BRANCH * D_MODEL,)
_REF_OFFS = np.concatenate([[0], np.cumsum(_REF_SIZES)])[:-1]
_NEW_OFFS = (OFF_HGRN, OFF_HGRN + 512, OFF_HGRN + 1024, OFF_HGRN + 1536, OFF_SWA_Q, OFF_SWA_K, OFF_SWA_V, OFF_S5,
             OFF_DIL_Q, OFF_DIL_K, OFF_DIL_V, OFF_GATE)

VMEM_LIMIT = 56 * 1024 * 1024


def _column_permutation():
    perm = np.zeros((D_IN,), np.int32)
    for size, ro, no in zip(_REF_SIZES, _REF_OFFS, _NEW_OFFS):
        perm[no:no + size] = np.arange(ro, ro + size)
    return perm


def _cparams(sem):
    return pltpu.CompilerParams(dimension_semantics=sem, vmem_limit_bytes=VMEM_LIMIT)


def _sigmoid(x):
    return 1.0 / (1.0 + jnp.exp(-x))


def _split3(x):
    hi = x.astype(BF16)
    r1 = x - hi.astype(F32)
    mid = r1.astype(BF16)
    lo = (r1 - mid.astype(F32)).astype(BF16)
    return hi, mid, lo


def _row_bcast(ref, r, n):
    return jnp.broadcast_to(ref[pl.ds(r, 1), :], (n, ref.shape[1]))


def _dot(a, b):
    return jnp.dot(a, b, preferred_element_type=F32)


def _dot_nt(a, b):
    return lax.dot_general(a, b, (((1,), (1,)), ((), ())), preferred_element_type=F32)


def _in_proj_kernel(x_ref, g_ref, w_ref, o_ref, hn_ref):
    @pl.when(pl.program_id(1) == 0)
    def _():
        x = x_ref[...]
        ms = jnp.mean(x * x, axis=-1, keepdims=True)
        hn_ref[...] = (x * lax.rsqrt(ms + EPS) * g_ref[...]).astype(BF16)

    o_ref[...] = _dot(hn_ref[...], w_ref[...])


def _in_proj(h2d, g, w_bf16, tm=1024, tn=512):
    t = h2d.shape[0]
    return pl.pallas_call(
        _in_proj_kernel,
        out_shape=jax.ShapeDtypeStruct((t, D_IN), F32),
        grid=(t // tm, D_IN // tn),
        in_specs=[pl.BlockSpec((tm, D_MODEL), lambda i, j: (i, 0)),
                  pl.BlockSpec((1, D_MODEL), lambda i, j: (0, 0)),
                  pl.BlockSpec((D_MODEL, tn), lambda i, j: (0, j))],
        out_specs=pl.BlockSpec((tm, tn), lambda i, j: (i, j)),
        scratch_shapes=[pltpu.VMEM((tm, D_MODEL), BF16)],
        compiler_params=_cparams(("parallel", "arbitrary")),
        name="in_proj",
    )(h2d, g.reshape(1, D_MODEL), w_bf16)


def _rope_tables(seq):
    inv_freq = ROPE_THETA ** (-jnp.arange(0, ROT_DIM, 2, dtype=F32) / ROT_DIM)
    ang = jnp.arange(seq, dtype=F32)[:, None] * inv_freq[None, :]
    cos, sin = jnp.cos(ang), jnp.sin(ang)
    half = ROT_DIM // 2
    rest = HEAD_DIM - ROT_DIM
    one, zero = jnp.ones((seq, rest), F32), jnp.zeros((seq, rest), F32)
    zh = jnp.zeros((seq, half), F32)
    cos_h = jnp.concatenate([cos, cos, one], axis=1)
    sup_h = jnp.concatenate([-sin, zh, zero], axis=1)
    sdn_h = jnp.concatenate([zh, sin, zero], axis=1)
    reps = LANES // HEAD_DIM
    return tuple(jnp.tile(t, (1, reps)) for t in (cos_h, sup_h, sdn_h))


def _rope(x, cos, sup, sdn):
    half = ROT_DIM // 2
    outs = []
    for p in range(x.shape[1] // LANES):
        xp = x[:, p * LANES:(p + 1) * LANES]
        outs.append(xp * cos + pltpu.roll(xp, LANES - half, 1) * sup + pltpu.roll(xp, half, 1) * sdn)
    return outs


def _band_attn_kernel(*refs, dil, nq, max_dist, has_sink, gqa, rows_per_pass):
    if has_sink:
        sink_ref, refs = refs[0], refs[1:]
    (q_ref, kc_ref, kp_ref, vc_ref, vp_ref,
     cc_ref, uc_ref, dc_ref, cp_ref, up_ref, dp_ref) = refs[:11]
    if has_sink:
        o_ref, q_sc, k_sc = refs[11:]
        lse_ref = None
    else:
        o_ref, lse_ref, q_sc, k_sc = refs[11:]
    j = pl.program_id(1)
    pair = pl.program_id(2)
    blk = BLOCK * dil
    span = nq * blk
    scale = HEAD_DIM ** -0.5

    lane = lax.broadcasted_iota(jnp.int32, (1, LANES), 1)
    lo_half = lane < HEAD_DIM

    def pick_kv(t):
        if not gqa:
            return t
        sw = pltpu.roll(t, HEAD_DIM, 1)
        return jnp.where(pair < SWA_Q_HEADS // (2 * SWA_KV_HEADS), jnp.where(lo_half, t, sw), jnp.where(lo_half, sw, t))

    for r0 in range(0, span, rows_per_pass):
        rs = slice(r0, r0 + rows_per_pass)
        q_sc[rs, :] = _rope(q_ref[rs, :], cc_ref[rs, :], uc_ref[rs, :], dc_ref[rs, :])[0] * scale
        k_sc[blk + r0:blk + r0 + rows_per_pass, :] = pick_kv(
            _rope(kc_ref[rs, :], cc_ref[rs, :], uc_ref[rs, :], dc_ref[rs, :])[0])
    pp = min(rows_per_pass, blk)
    for r0 in range(0, blk, pp):
        rs = slice(r0, r0 + pp)
        k_sc[rs, :] = pick_kv(_rope(kp_ref[rs, :], cp_ref[rs, :], up_ref[rs, :], dp_ref[rs, :])[0])

    qi = lax.broadcasted_iota(jnp.int32, (BLOCK, 2 * BLOCK), 0)
    ki = lax.broadcasted_iota(jnp.int32, (BLOCK, 2 * BLOCK), 1)
    dist = qi + BLOCK - ki
    band = (dist >= 0) & (dist <= max_dist)

    def rows(start, size):
        return pl.ds(start, size) if dil == 1 else pl.ds(start, size, stride=dil)

    for res in range(dil):
        for i in range(nq):
            valid = band & (ki >= jnp.where(j == 0, BLOCK, 0)) if i == 0 else band
            q_rows = rows(i * blk + res, BLOCK)
            qp = q_sc[q_rows, :]
            kp = k_sc[rows(i * blk + res, 2 * BLOCK), :].astype(BF16)
            v_prev = vp_ref[rows(res, BLOCK), :] if i == 0 else vc_ref[rows((i - 1) * blk + res, BLOCK), :]
            vp = pick_kv(jnp.concatenate([v_prev, vc_ref[q_rows, :]], axis=0)).astype(BF16)
            outs, lses = [], []
            for half in range(2):
                sel = lo_half if half == 0 else jnp.logical_not(lo_half)
                qm = jnp.where(sel, qp, 0.0).astype(BF16)
                s = _dot_nt(qm, kp)
                s = jnp.where(valid, s, MASK_VALUE)
                m = jnp.max(s, axis=-1, keepdims=True)
                if has_sink:
                    sk = sink_ref[2 * pair + half]
                    m = jnp.maximum(m, sk)
                pe = jnp.exp(s - m)
                den = jnp.sum(pe, axis=-1, keepdims=True)
                if has_sink:
                    den = den + jnp.exp(sk - m)
                outs.append(_dot(pe.astype(BF16), vp) / den)
                lses.append(m + jnp.log(den))
            o_ref[q_rows, :] = jnp.where(lo_half, outs[0], outs[1])
            if lse_ref is not None:
                lse_ref[q_rows, :] = jnp.where(lo_half, lses[0], lses[1])


def _band_attn(proj, tables, q_off, k_off, v_off, width, kv_width, dil, nq, max_dist, sinks=None):
    bsz, seq, _ = proj.shape
    blk = BLOCK * dil
    span = nq * blk
    gqa = kv_width != width
    assert kv_width == (LANES if gqa else width)
    n_pairs = width // LANES
    rows_per_pass = min(256, span)
    kv_col = (lambda off: (lambda p: off // LANES)) if gqa else (lambda off: (lambda p: off // LANES + p))
    cur = lambda col: (lambda b, j, p: (b, j, col(p)))
    prev = lambda col: (lambda b, j, p: (b, jnp.maximum(j * nq - 1, 0), col(p)))
    tcur = lambda b, j, p: (j, 0)
    tprev = lambda b, j, p: (jnp.maximum(j * nq - 1, 0), 0)
    in_specs = [pl.BlockSpec((None, span, LANES), cur(lambda p: q_off // LANES + p)),
                pl.BlockSpec((None, span, LANES), cur(kv_col(k_off))),
                pl.BlockSpec((None, blk, LANES), prev(kv_col(k_off))),
                pl.BlockSpec((None, span, LANES), cur(kv_col(v_off))),
                pl.BlockSpec((None, blk, LANES), prev(kv_col(v_off)))]
    in_specs += [pl.BlockSpec((span, LANES), tcur)] * 3 + [pl.BlockSpec((blk, LANES), tprev)] * 3
    args = [proj] * 5 + list(tables) + list(tables)
    has_sink = sinks is not None
    if has_sink:
        in_specs = [pl.BlockSpec(memory_space=pltpu.SMEM)] + in_specs
        args = [sinks] + args
    o_spec = pl.BlockSpec((None, span, LANES), lambda b, j, p: (b, j, p))
    o_shape = jax.ShapeDtypeStruct((bsz, seq, width), F32)
    out_shape, out_specs = (o_shape, o_spec) if has_sink else ((o_shape, o_shape), (o_spec, o_spec))
    return pl.pallas_call(
        functools.partial(_band_attn_kernel, dil=dil, nq=nq, max_dist=max_dist, has_sink=has_sink, gqa=gqa,
                          rows_per_pass=rows_per_pass),
        out_shape=out_shape,
        grid=(bsz, seq // span, n_pairs),
        in_specs=in_specs,
        out_specs=out_specs,
        scratch_shapes=[pltpu.VMEM((span, LANES), F32), pltpu.VMEM((span + blk, LANES), F32)],
        compiler_params=_cparams(("parallel", "arbitrary", "arbitrary")),
        name="swa_attn" if has_sink else f"dil_attn_{dil}",
    )(*args)


def _hgrn_kernel(q_ref, f_ref, i_ref, g_ref, lb_ref, ng_ref, ed_ref, o_ref, st_ref, b_sc, k_sc):
    L, C, D = HGRN_CHUNK, HGRN_SUB, HGRN_DIM

    @pl.when(pl.program_id(1) == 0)
    def _():
        st_ref[...] = jnp.zeros_like(st_ref)

    row = lax.broadcasted_iota(jnp.int32, (L, L), 0)
    col = lax.broadcasted_iota(jnp.int32, (L, L), 1)
    tri = (col <= row).astype(BF16)
    rowv = lax.broadcasted_iota(jnp.int32, (L, D), 0)
    same_block = lambda size: (row >> int(math.log2(size))) == (col >> int(math.log2(size)))
    diag_mask = same_block(C) & (col <= row)

    for hd in range(HGRN_HEADS):
        cs = slice(hd * D, (hd + 1) * D)
        x = f_ref[:, cs]
        lb = lb_ref[:, cs]
        log_sig = jnp.minimum(x, 0.0) - jnp.log1p(jnp.exp(-jnp.abs(x)))
        t0 = jnp.log(jnp.maximum(lb, LB_FLOOR))
        t1 = jnp.log1p(-lb) + log_sig
        log_f = jnp.maximum(t0, t1) + jnp.log1p(jnp.exp(-jnp.abs(t0 - t1)))
        kf = (1.0 - lb) * _sigmoid(-x)
        qx = q_ref[:, cs]
        qf = qx * _sigmoid(qx)
        v = i_ref[:, cs]
        vb = v.astype(BF16)

        hi, mid, lo = _split3(log_f)
        b = _dot(tri, hi) + _dot(tri, mid) + _dot(tri, lo)
        b_sc[...] = b
        k_sc[...] = kf

        scores = jnp.zeros((L, L), F32)
        m = C
        while m < L:
            ref_rows = [_row_bcast(b_sc, s0 + m - 1, 2 * m) for s0 in range(0, L, 2 * m)]
            ref = ref_rows[0] if len(ref_rows) == 1 else jnp.concatenate(ref_rows, axis=0)
            upper = ((rowv >> int(math.log2(m))) & 1) == 1
            w = jnp.exp(jnp.where(upper, b - ref, ref - b))
            qd = jnp.where(upper, qf * w, 0.0).astype(BF16)
            kd = jnp.where(upper, 0.0, kf * w).astype(BF16)
            scores = scores + jnp.where(same_block(2 * m), _dot_nt(qd, kd), 0.0)
            m *= 2

        xs = []
        for s in range(C):
            bs = jnp.concatenate([_row_bcast(b_sc, blk0 + s, C) for blk0 in range(0, L, C)], axis=0)
            ks = jnp.concatenate([_row_bcast(k_sc, blk0 + s, C) for blk0 in range(0, L, C)], axis=0)
            xs.append((qf * jnp.exp(jnp.minimum(b - bs, 0.0)) * ks).astype(BF16))
        diag = _dot(jnp.concatenate(xs, axis=1), ed_ref[...])
        scores = scores + jnp.where(diag_mask, diag, 0.0)

        st = st_ref[hd]
        o = _dot(scores.astype(BF16), vb) + _dot_nt((qf * jnp.exp(b)).astype(BF16), st.astype(BF16))

        b_last = _row_bcast(b_sc, L - 1, L)
        kd_end = (kf * jnp.exp(b_last - b)).astype(BF16)
        st_ref[hd] = st * jnp.exp(b_last) + _dot(v.T.astype(BF16), kd_end)

        ms = jnp.mean(o * o, axis=-1, keepdims=True)
        gx = g_ref[:, cs]
        o_ref[:, cs] = o * lax.rsqrt(ms + EPS) * ng_ref[...] * (gx * _sigmoid(gx))


def _diag_sum_matrix():
    c, d = HGRN_SUB, HGRN_DIM
    rows_s = np.arange(c * d) // d
    cols = np.arange(HGRN_CHUNK) % c
    return jnp.asarray((rows_s[:, None] == cols[None, :]).astype(np.float32), dtype=BF16)


def _hgrn(proj, lower_bound, norm_g):
    bsz, seq, _ = proj.shape
    L = HGRN_CHUNK
    base = OFF_HGRN // HGRN_WIDTH
    col = lambda c: (lambda b, j: (b, j, base + c))
    const = lambda b, j: (0, 0)
    return pl.pallas_call(
        _hgrn_kernel,
        out_shape=jax.ShapeDtypeStruct((bsz, seq, HGRN_WIDTH), F32),
        grid=(bsz, seq // L),
        in_specs=[pl.BlockSpec((None, L, HGRN_WIDTH), col(c)) for c in range(4)]
        + [pl.BlockSpec((1, HGRN_WIDTH), const), pl.BlockSpec((1, HGRN_DIM), const),
           pl.BlockSpec((HGRN_SUB * HGRN_DIM, L), const)],
        out_specs=pl.BlockSpec((None, L, HGRN_WIDTH), lambda b, j: (b, j, 0)),
        scratch_shapes=[pltpu.VMEM((HGRN_HEADS, HGRN_DIM, HGRN_DIM), F32),
                        pltpu.VMEM((L, HGRN_DIM), F32), pltpu.VMEM((L, HGRN_DIM), F32)],
        compiler_params=_cparams(("parallel", "arbitrary")),
        name="hgrn2",
    )(proj, proj, proj, proj, lower_bound.reshape(1, HGRN_WIDTH), norm_g.reshape(1, HGRN_DIM), _diag_sum_matrix())


def _s5_kernel(u0_ref, u1_ref, u2_ref, u3_ref, pi_ref, pit_ref, wb_ref, wc_ref, ar_ref, ai_ref, alr_ref, ali_ref,
               d_ref, gw_ref, gb_ref, o_ref, xs, carry):
    tm = S5_TILE
    lc = tm // S5_LANES
    half = S5_WIDTH
    u_refs = (u0_ref, u1_ref, u2_ref, u3_ref)

    @pl.when(pl.program_id(1) == 0)
    def _():
        carry[...] = jnp.zeros_like(carry)

    pi = pi_ref[...]
    for c in range(S5_COLS):
        up = _dot(pi, u_refs[c][...].astype(BF16)).astype(BF16)
        xs[:, c * 2 * half:(c + 1) * 2 * half] = _dot(up, wb_ref[c])

    def cmul_add(ar, ai, xr, xi, br, bi):
        return ar * xr - ai * xi + br, ar * xi + ai * xr + bi

    for c in range(S5_COLS):
        re = slice(c * 2 * half, c * 2 * half + half)
        im = slice(c * 2 * half + half, (c + 1) * 2 * half)
        ar = jnp.broadcast_to(ar_ref[:, c * half:(c + 1) * half], (S5_LANES, half))
        ai = jnp.broadcast_to(ai_ref[:, c * half:(c + 1) * half], (S5_LANES, half))

        def step(tau, st, re=re, im=im, ar=ar, ai=ai):
            r0 = pl.multiple_of(tau * S5_LANES, S5_LANES)
            nr, ni = cmul_add(ar, ai, st[0], st[1], xs[pl.ds(r0, S5_LANES), re], xs[pl.ds(r0, S5_LANES), im])
            xs[pl.ds(r0, S5_LANES), re] = nr
            xs[pl.ds(r0, S5_LANES), im] = ni
            return nr, ni

        zero = jnp.zeros((S5_LANES, half), F32)
        lax.fori_loop(0, lc, step, (zero, zero), unroll=4)

    last = (lc - 1) * S5_LANES
    sub_id = lax.broadcasted_iota(jnp.int32, (S5_LANES, half), 0)
    for c in range(S5_COLS):
        re = slice(c * 2 * half, c * 2 * half + half)
        im = slice(c * 2 * half + half, (c + 1) * 2 * half)
        alr, ali = alr_ref[:, c * half:(c + 1) * half], ali_ref[:, c * half:(c + 1) * half]
        gr, gi = carry[0:1, re], carry[0:1, im]
        g_re = jnp.zeros((S5_LANES, half), F32)
        g_im = jnp.zeros((S5_LANES, half), F32)
        for sub in range(S5_LANES):
            g_re = jnp.where(sub_id == sub, gr, g_re)
            g_im = jnp.where(sub_id == sub, gi, g_im)
            gr, gi = cmul_add(alr, ali, gr, gi, xs[pl.ds(last + sub, 1), re], xs[pl.ds(last + sub, 1), im])
        carry[0:1, re] = gr
        carry[0:1, im] = gi
        ar = jnp.broadcast_to(ar_ref[:, c * half:(c + 1) * half], (S5_LANES, half))
        ai = jnp.broadcast_to(ai_ref[:, c * half:(c + 1) * half], (S5_LANES, half))

        def fix(tau, st, re=re, im=im, ar=ar, ai=ai):
            r0 = pl.multiple_of(tau * S5_LANES, S5_LANES)
            cr, ci = cmul_add(ar, ai, st[0], st[1], 0.0, 0.0)
            xs[pl.ds(r0, S5_LANES), re] = xs[pl.ds(r0, S5_LANES), re] + cr
            xs[pl.ds(r0, S5_LANES), im] = xs[pl.ds(r0, S5_LANES), im] + ci
            return cr, ci

        lax.fori_loop(0, lc, fix, (g_re, g_im), unroll=4)

    pit = pit_ref[...]
    ys = []
    for c in range(S5_COLS):
        yp = _dot(xs[:, c * 2 * half:(c + 1) * 2 * half].astype(BF16), wc_ref[c])
        hi, mid, lo = _split3(yp)
        y = _dot(pit, hi) + _dot(pit, mid) + _dot(pit, lo)
        ys.append(y + d_ref[:, c * LANES:(c + 1) * LANES] * u_refs[c][...])
    y = jnp.concatenate(ys, axis=1)
    y = y * (0.5 * (1.0 + jnp.tanh(math.sqrt(2.0 / math.pi) * (y + 0.044715 * (y * y * y)))))
    o_ref[...] = y * _sigmoid(_dot(y.astype(BF16), gw_ref[...]) + gb_ref[...])


def _s5_permutation():
    tm, lc = S5_TILE, S5_TILE // S5_LANES
    rho = np.arange(tm)
    src = (rho % S5_LANES) * lc + rho // S5_LANES
    pi = np.zeros((tm, tm), np.float32)
    pi[rho, src] = 1.0
    return jnp.asarray(pi, dtype=BF16), jnp.asarray(pi.T, dtype=BF16)


def _s5_params(a_re, a_im, b_re, b_im, c_re, c_im, log_step):
    a = lax.complex(a_re.astype(F32), a_im.astype(F32))
    dt = jnp.exp(log_step.astype(F32))[:, None]
    a_bar = jnp.exp(dt * a)
    b_bar = ((a_bar - 1.0) / a)[..., None] * lax.complex(b_re.astype(F32), b_im.astype(F32))
    a_l = a_bar ** (S5_TILE // S5_LANES)
    gpc = S5_GROUPS // S5_COLS
    eye = jnp.eye(gpc, dtype=F32)

    def in_w(t):
        t = t.reshape(S5_COLS, gpc, S5_STATE, S5_GROUP)
        return jnp.einsum('cgpi,gh->cgihp', t, eye).reshape(S5_COLS, gpc * S5_GROUP, gpc * S5_STATE)

    def out_w(t):
        t = t.reshape(S5_COLS, gpc, S5_GROUP, S5_STATE)
        return jnp.einsum('cgip,gh->cgphi', t, eye).reshape(S5_COLS, gpc * S5_STATE, gpc * S5_GROUP)

    wb = jnp.concatenate([in_w(jnp.real(b_bar)), in_w(jnp.imag(b_bar))], axis=2).astype(BF16)
    wc = jnp.concatenate([out_w(c_re.astype(F32)), -out_w(c_im.astype(F32))], axis=1).astype(BF16)
    flat = lambda t: t.reshape(1, S5_GROUPS * S5_STATE)
    return wb, wc, flat(jnp.real(a_bar)), flat(jnp.imag(a_bar)), flat(jnp.real(a_l)), flat(jnp.imag(a_l))


def _s5(proj, params, d, glu_w, glu_b):
    bsz, seq, _ = proj.shape
    tm = S5_TILE
    wb, wc, ar, ai, alr, ali = params
    pi, pit = _s5_permutation()
    nstate = S5_GROUPS * S5_STATE
    base = OFF_S5 // LANES
    const2 = lambda b, j: (0, 0)
    const3 = lambda b, j: (0, 0, 0)
    in_specs = [pl.BlockSpec((None, tm, LANES), (lambda c: (lambda b, j: (b, j, base + c)))(c)) for c in range(S5_COLS)]
    in_specs += [pl.BlockSpec((tm, tm), const2), pl.BlockSpec((tm, tm), const2),
                 pl.BlockSpec((S5_COLS, LANES, 2 * S5_WIDTH), const3), pl.BlockSpec((S5_COLS, 2 * S5_WIDTH, LANES), const3)]
    in_specs += [pl.BlockSpec((1, nstate), const2)] * 4
    in_specs += [pl.BlockSpec((1, S5_WIDTH), const2), pl.BlockSpec((S5_WIDTH, S5_WIDTH), const2),
                 pl.BlockSpec((1, S5_WIDTH), const2)]
    return pl.pallas_call(
        _s5_kernel,
        out_shape=jax.ShapeDtypeStruct((bsz, seq, S5_WIDTH), F32),
        grid=(bsz, seq // tm),
        in_specs=in_specs,
        out_specs=pl.BlockSpec((None, tm, S5_WIDTH), lambda b, j: (b, j, 0)),
        scratch_shapes=[pltpu.VMEM((tm, 2 * nstate), F32), pltpu.VMEM((S5_LANES, 2 * nstate), F32)],
        compiler_params=_cparams(("parallel", "arbitrary")),
        name="s5_ssm",
    )(proj, proj, proj, proj, pi, pit, wb, wc, ar, ai, alr, ali, d.reshape(1, S5_WIDTH), glu_w.astype(BF16),
      glu_b.reshape(1, S5_WIDTH))


def _merge_kernel(h_ref, g0_ref, g1_ref, g2_ref, g3_ref, ya_ref, yb_ref, yc_ref,
                  o1_ref, o2_ref, o3_ref, l1_ref, l2_ref, l3_ref, wa_ref, wb_ref, wc_ref, wd_ref, wo_ref, out_ref):
    l1, l2, l3 = l1_ref[...], l2_ref[...], l3_ref[...]
    mx = jnp.maximum(jnp.maximum(l1, l2), l3)
    e1, e2, e3 = jnp.exp(l1 - mx), jnp.exp(l2 - mx), jnp.exp(l3 - mx)
    tot = e1 + e2 + e3
    yd = (e1 / tot) * o1_ref[...] + (e2 / tot) * o2_ref[...] + (e3 / tot) * o3_ref[...]
    merged = (_sigmoid(g0_ref[...]) * _dot(ya_ref[...].astype(BF16), wa_ref[...])
              + _sigmoid(g1_ref[...]) * _dot(yb_ref[...].astype(BF16), wb_ref[...])
              + _sigmoid(g2_ref[...]) * _dot(yc_ref[...].astype(BF16), wc_ref[...])
              + _sigmoid(g3_ref[...]) * _dot(yd.astype(BF16), wd_ref[...]))
    out_ref[...] = h_ref[...] + _dot(merged.astype(BF16), wo_ref[...])


def _merge(h2d, proj2d, ya, yb, yc, dil, wa, wb, wc, wd, wo, tm=256):
    t = h2d.shape[0]
    row = lambda i: (i, 0)
    const = lambda i: (0, 0)
    gate = lambda c: (lambda i: (i, OFF_GATE // D_MODEL + c))
    in_specs = [pl.BlockSpec((tm, D_MODEL), row)]
    in_specs += [pl.BlockSpec((tm, D_MODEL), gate(c)) for c in range(N_BRANCH)]
    in_specs += [pl.BlockSpec((tm, y.shape[1]), row) for y in (ya, yb, yc)]
    in_specs += [pl.BlockSpec((tm, DIL_OUT), row)] * 6
    in_specs += [pl.BlockSpec(w.shape, const) for w in (wa, wb, wc, wd, wo)]
    (o1, s1), (o2, s2), (o3, s3) = dil
    return pl.pallas_call(
        _merge_kernel,
        out_shape=jax.ShapeDtypeStruct((t, D_MODEL), F32),
        grid=(t // tm,),
        in_specs=in_specs,
        out_specs=pl.BlockSpec((tm, D_MODEL), row),
        compiler_params=_cparams(("parallel",)),
        name="branch_merge",
    )(h2d, proj2d, proj2d, proj2d, proj2d, ya, yb, yc, o1, o2, o3, s1, s2, s3, wa, wb, wc, wd, wo)


def _ffn_kernel(h_ref, hp_ref, g_ref, wa_ref, wb_ref, cwa_ref, cwb_ref, cba_ref, cbb_ref, wd_ref, fg_ref, o_ref,
                hn_sc, acc_sc, *, final_norm):
    i, k = pl.program_id(1), pl.program_id(2)
    tm, halo = FFN_TM, FFN_HALO

    def norm(x):
        ms = jnp.mean(x * x, axis=-1, keepdims=True)
        return x * lax.rsqrt(ms + EPS) * g_ref[...]

    @pl.when(k == 0)
    def _():
        hn_sc[halo:, :] = norm(h_ref[...]).astype(BF16)
        hn_sc[:halo, :] = jnp.where(i == 0, 0.0, norm(hp_ref[...])).astype(BF16)
        acc_sc[...] = jnp.zeros_like(acc_sc)

    hn = hn_sc[...]

    def conv(w_ref, cw_ref, cb_ref):
        u = _dot(hn, w_ref[...])
        u1 = pltpu.roll(u, 1, 0)
        u2 = pltpu.roll(u, 2, 0)
        return (cw_ref[0:1, :] * u2[halo:, :] + cw_ref[1:2, :] * u1[halo:, :] + cw_ref[2:3, :] * u[halo:, :]
                + cb_ref[...])

    a = conv(wa_ref, cwa_ref, cba_ref)
    b = conv(wb_ref, cwb_ref, cbb_ref)
    z = (a * _sigmoid(a)) * b
    acc_sc[...] += _dot(z.astype(BF16), wd_ref[...])

    @pl.when(k == pl.num_programs(2) - 1)
    def _():
        y = h_ref[...] + acc_sc[...]
        if final_norm:
            ms = jnp.mean(y * y, axis=-1, keepdims=True)
            y = y * lax.rsqrt(ms + EPS) * fg_ref[...]
        o_ref[...] = y


def _ffn(h, g, w_up, conv_w, conv_b, w_down, final_g, final_norm):
    bsz, seq, _ = h.shape
    tm, tf, halo = FFN_TM, FFN_TF, FFN_HALO
    nk = D_FF // tf
    const = lambda b, i, k: (0, 0)
    return pl.pallas_call(
        functools.partial(_ffn_kernel, final_norm=final_norm),
        out_shape=jax.ShapeDtypeStruct((bsz, seq, D_MODEL), F32),
        grid=(bsz, seq // tm, nk),
        in_specs=[pl.BlockSpec((None, tm, D_MODEL), lambda b, i, k: (b, i, 0)),
                  pl.BlockSpec((None, halo, D_MODEL), lambda b, i, k: (b, jnp.maximum(i * (tm // halo) - 1, 0), 0)),
                  pl.BlockSpec((1, D_MODEL), const),
                  pl.BlockSpec((D_MODEL, tf), lambda b, i, k: (0, k)),
                  pl.BlockSpec((D_MODEL, tf), lambda b, i, k: (0, nk + k)),
                  pl.BlockSpec((CONV_WIDTH, tf), lambda b, i, k: (0, k)),
                  pl.BlockSpec((CONV_WIDTH, tf), lambda b, i, k: (0, nk + k)),
                  pl.BlockSpec((1, tf), lambda b, i, k: (0, k)),
                  pl.BlockSpec((1, tf), lambda b, i, k: (0, nk + k)),
                  pl.BlockSpec((tf, D_MODEL), lambda b, i, k: (k, 0)),
                  pl.BlockSpec((1, D_MODEL), const)],
        out_specs=pl.BlockSpec((None, tm, D_MODEL), lambda b, i, k: (b, i, 0)),
        scratch_shapes=[pltpu.VMEM((tm + halo, D_MODEL), BF16), pltpu.VMEM((tm, D_MODEL), F32)],
        compiler_params=_cparams(("parallel", "arbitrary", "arbitrary")),
        name="conv_ffn",
    )(h, h, g.reshape(1, D_MODEL), w_up, w_up, conv_w, conv_w, conv_b.reshape(1, -1), conv_b.reshape(1, -1), w_down,
      final_g.reshape(1, D_MODEL))


def kernel(x, norm1_g, w_in, hgrn_lb_logits, hgrn_norm_g, attn_sinks, s5_a_re, s5_a_im, s5_b_re, s5_b_im, s5_c_re, s5_c_im, s5_d, s5_log_step, s5_glu_w, s5_glu_b, w_branch_a, w_branch_b, w_branch_c, w_branch_d, w_out, norm2_g, ffn_w_up, ffn_conv_w, ffn_conv_b, ffn_w_down, final_norm_g):
    bsz, seq, _ = x.shape
    tables = _rope_tables(seq)
    p = jax.nn.softmax(hgrn_lb_logits.astype(F32), axis=0)
    lower_bounds = jnp.cumsum(p, axis=0) - p[0]
    perm = _column_permutation()

    h = x
    for l in range(DEPTH):
        w_in_l = jnp.take(w_in[l], perm, axis=1).astype(BF16)
        proj2d = _in_proj(h.reshape(bsz * seq, D_MODEL), norm1_g[l], w_in_l)
        proj = proj2d.reshape(bsz, seq, D_IN)

        y_a = _hgrn(proj, lower_bounds[l], hgrn_norm_g[l])
        y_b = _band_attn(proj, tables, OFF_SWA_Q, OFF_SWA_K, OFF_SWA_V, SWA_Q_WIDTH, SWA_KV_WIDTH, dil=1, nq=8,
                         max_dist=SWA_WINDOW - 1, sinks=attn_sinks[l].astype(F32))
        y_c = _s5(proj, _s5_params(s5_a_re[l], s5_a_im[l], s5_b_re[l], s5_b_im[l], s5_c_re[l], s5_c_im[l],
                                   s5_log_step[l]), s5_d[l], s5_glu_w[l], s5_glu_b[l])
        dil = []
        for g, (window, dilation) in enumerate(DIL_PAIRS):
            o, lse = _band_attn(proj, tables, OFF_DIL_Q + g * DIL_OUT, OFF_DIL_K + g * DIL_OUT,
                                OFF_DIL_V + g * DIL_OUT, DIL_OUT, DIL_OUT, dil=dilation, nq=DIL_NQ[g],
                                max_dist=window // dilation)
            dil.append((o.reshape(bsz * seq, DIL_OUT), lse.reshape(bsz * seq, DIL_OUT)))

        flat = lambda t: t.reshape(bsz * seq, t.shape[-1])
        h2d = _merge(h.reshape(bsz * seq, D_MODEL), proj2d, flat(y_a), flat(y_b), flat(y_c), dil,
                     w_branch_a[l].astype(BF16), w_branch_b[l].astype(BF16), w_branch_c[l].astype(BF16),
                     w_branch_d[l].astype(BF16), w_out[l].astype(BF16))
        h = _ffn(h2d.reshape(bsz, seq, D_MODEL), norm2_g[l], ffn_w_up[l].astype(BF16), ffn_conv_w[l], ffn_conv_b[l],
                 ffn_w_down[l].astype(BF16), final_norm_g, final_norm=(l == DEPTH - 1))
    return h
```

```python
import functools
import math

import numpy as np
import jax
import jax.numpy as jnp
from jax import lax
from jax.experimental import pallas as pl
from jax.experimental.pallas import tpu as pltpu

F32 = jnp.float32
BF16 = jnp.bfloat16

D_MODEL = 1024
DEPTH = 2
HEAD_DIM = 64
BLOCK = 128
ROPE_THETA = 500000.0
ROT_DIM = HEAD_DIM // 4
EPS = 1e-6
MASK_VALUE = -1e30
LB_FLOOR = 1e-30
N_BRANCH = 4

HGRN_HEADS = 4
HGRN_DIM = 128
HGRN_WIDTH = HGRN_HEADS * HGRN_DIM
HGRN_CHUNK = 128
HGRN_SUB = 16

SWA_Q_HEADS = 8
SWA_KV_HEADS = 2
SWA_WINDOW = 128
SWA_Q_WIDTH = SWA_Q_HEADS * HEAD_DIM
SWA_KV_WIDTH = SWA_KV_HEADS * HEAD_DIM

S5_WIDTH = 512
S5_GROUP = 16
S5_GROUPS = S5_WIDTH // S5_GROUP
S5_STATE = 64
S5_COLS = 4
S5_TILE = 512
S5_LANES = 8

DIL_PAIRS = ((128, 1), (512, 4), (2048, 16))
DIL_HEADS_PER_GROUP = 4
DIL_HEADS = DIL_HEADS_PER_GROUP * len(DIL_PAIRS)
DIL_OUT = DIL_HEADS_PER_GROUP * HEAD_DIM
DIL_NQ = (8, 2, 1)

D_FF = 2816
CONV_WIDTH = 3
FFN_TF = 256
FFN_TM = 512
FFN_HALO = 16

LANES = 128

OFF_HGRN = 0
OFF_SWA_Q = OFF_HGRN + 4 * HGRN_WIDTH
OFF_S5 = OFF_SWA_Q + SWA_Q_WIDTH
OFF_DIL_Q = OFF_S5 + S5_WIDTH
OFF_DIL_K = OFF_DIL_Q + DIL_HEADS * HEAD_DIM
OFF_DIL_V = OFF_DIL_K + DIL_HEADS * HEAD_DIM
OFF_SWA_K = OFF_DIL_V + DIL_HEADS * HEAD_DIM
OFF_SWA_V = OFF_SWA_K + SWA_KV_WIDTH
D_IN = OFF_SWA_V + SWA_KV_WIDTH

_REF_SIZES = (HGRN_WIDTH,) * 4 + (SWA_Q_WIDTH, SWA_KV_WIDTH, SWA_KV_WIDTH, S5_WIDTH) + (DIL_HEADS * HEAD_DIM,) * 3 + (N_BRANCH * D_MODEL,)
_REF_OFFS = np.concatenate([[0], np.cumsum(_REF_SIZES)])[:-1]
_NEW_OFFS = (OFF_HGRN, OFF_HGRN + 512, OFF_HGRN + 1024, OFF_HGRN + 1536, OFF_SWA_Q, OFF_SWA_K, OFF_SWA_V, OFF_S5,
             OFF_DIL_Q, OFF_DIL_K, OFF_DIL_V)
REF_GATE_OFF = int(_REF_OFFS[-1])

VMEM_LIMIT = 56 * 1024 * 1024


def _column_permutation():
    perm = np.zeros((D_IN,), np.int32)
    for size, ro, no in zip(_REF_SIZES[:-1], _REF_OFFS[:-1], _NEW_OFFS):
        perm[no:no + size] = np.arange(ro, ro + size)
    return perm


def _cparams(sem):
    return pltpu.CompilerParams(dimension_semantics=sem, vmem_limit_bytes=VMEM_LIMIT)


def _resident(shape):
    return pl.BlockSpec(shape, lambda *_: (0,) * len(shape), pipeline_mode=pl.Buffered(1))


def _sigmoid(x):
    return 1.0 / (1.0 + jnp.exp(-x))


def _split3(x):
    hi = x.astype(BF16)
    r1 = x - hi.astype(F32)
    mid = r1.astype(BF16)
    lo = (r1 - mid.astype(F32)).astype(BF16)
    return hi, mid, lo


def _row_bcast(ref, r, n):
    return jnp.broadcast_to(ref[pl.ds(r, 1), :], (n, ref.shape[1]))


def _dot(a, b):
    return jnp.dot(a, b, preferred_element_type=F32)


def _dot_nt(a, b):
    return lax.dot_general(a, b, (((1,), (1,)), ((), ())), preferred_element_type=F32)


def _in_proj_kernel(x_ref, g_ref, w_ref, o_ref, hn_ref):
    @pl.when(pl.program_id(1) == 0)
    def _():
        x = x_ref[...]
        ms = jnp.mean(x * x, axis=-1, keepdims=True)
        hn_ref[...] = (x * lax.rsqrt(ms + EPS) * g_ref[...]).astype(BF16)

    o_ref[...] = _dot(hn_ref[...], w_ref[...])


def _in_proj(h2d, g, w_bf16, tm=1024, tn=D_IN // 2):
    t = h2d.shape[0]
    return pl.pallas_call(
        _in_proj_kernel,
        out_shape=jax.ShapeDtypeStruct((t, D_IN), F32),
        grid=(t // tm, D_IN // tn),
        in_specs=[pl.BlockSpec((tm, D_MODEL), lambda i, j: (i, 0)),
                  pl.BlockSpec((1, D_MODEL), lambda i, j: (0, 0)),
                  pl.BlockSpec((D_MODEL, tn), lambda i, j: (0, j))],
        out_specs=pl.BlockSpec((tm, tn), lambda i, j: (i, j)),
        scratch_shapes=[pltpu.VMEM((tm, D_MODEL), BF16)],
        compiler_params=_cparams(("parallel", "arbitrary")),
        name="in_proj",
    )(h2d, g.reshape(1, D_MODEL), w_bf16)


def _rope_tables(seq):
    inv_freq = ROPE_THETA ** (-jnp.arange(0, ROT_DIM, 2, dtype=F32) / ROT_DIM)
    ang = jnp.arange(seq, dtype=F32)[:, None] * inv_freq[None, :]
    cos, sin = jnp.cos(ang), jnp.sin(ang)
    half = ROT_DIM // 2
    rest = HEAD_DIM - ROT_DIM
    one, zero = jnp.ones((seq, rest), F32), jnp.zeros((seq, rest), F32)
    zh = jnp.zeros((seq, half), F32)
    cos_h = jnp.concatenate([cos, cos, one], axis=1)
    sup_h = jnp.concatenate([-sin, zh, zero], axis=1)
    sdn_h = jnp.concatenate([zh, sin, zero], axis=1)
    reps = LANES // HEAD_DIM
    return tuple(jnp.tile(t, (1, reps)) for t in (cos_h, sup_h, sdn_h))


def _rope(x, cos, sup, sdn):
    half = ROT_DIM // 2
    outs = []
    for p in range(x.shape[1] // LANES):
        xp = x[:, p * LANES:(p + 1) * LANES]
        outs.append(xp * cos + pltpu.roll(xp, LANES - half, 1) * sup + pltpu.roll(xp, half, 1) * sdn)
    return outs


def _band_attn_kernel(*refs, dil, nq, max_dist, has_sink, gqa, rows_per_pass):
    if has_sink:
        sink_ref, refs = refs[0], refs[1:]
    (q_ref, kc_ref, kp_ref, vc_ref, vp_ref,
     cc_ref, uc_ref, dc_ref, cp_ref, up_ref, dp_ref) = refs[:11]
    if has_sink:
        o_ref, q_sc, k_sc = refs[11:]
        lse_ref = None
    else:
        o_ref, lse_ref, q_sc, k_sc = refs[11:]
    j = pl.program_id(1)
    pair = pl.program_id(2)
    blk = BLOCK * dil
    span = nq * blk
    scale = HEAD_DIM ** -0.5

    lane = lax.broadcasted_iota(jnp.int32, (1, LANES), 1)
    lo_half = lane < HEAD_DIM

    def pick_kv(t):
        if not gqa:
            return t
        sw = pltpu.roll(t, HEAD_DIM, 1)
        return jnp.where(pair < SWA_Q_HEADS // (2 * SWA_KV_HEADS), jnp.where(lo_half, t, sw), jnp.where(lo_half, sw, t))

    for r0 in range(0, span, rows_per_pass):
        rs = slice(r0, r0 + rows_per_pass)
        q_sc[rs, :] = _rope(q_ref[rs, :], cc_ref[rs, :], uc_ref[rs, :], dc_ref[rs, :])[0] * scale
        k_sc[blk + r0:blk + r0 + rows_per_pass, :] = pick_kv(
            _rope(kc_ref[rs, :], cc_ref[rs, :], uc_ref[rs, :], dc_ref[rs, :])[0])
    pp = min(rows_per_pass, blk)
    for r0 in range(0, blk, pp):
        rs = slice(r0, r0 + pp)
        k_sc[rs, :] = pick_kv(_rope(kp_ref[rs, :], cp_ref[rs, :], up_ref[rs, :], dp_ref[rs, :])[0])

    qi = lax.broadcasted_iota(jnp.int32, (BLOCK, 2 * BLOCK), 0)
    ki = lax.broadcasted_iota(jnp.int32, (BLOCK, 2 * BLOCK), 1)
    dist = qi + BLOCK - ki
    band = (dist >= 0) & (dist <= max_dist)

    def rows(start, size):
        return pl.ds(start, size) if dil == 1 else pl.ds(start, size, stride=dil)

    for res in range(dil):
        for i in range(nq):
            valid = band & (ki >= jnp.where(j == 0, BLOCK, 0)) if i == 0 else band
            q_rows = rows(i * blk + res, BLOCK)
            qp = q_sc[q_rows, :]
            kp = k_sc[rows(i * blk + res, 2 * BLOCK), :].astype(BF16)
            v_prev = vp_ref[rows(res, BLOCK), :] if i == 0 else vc_ref[rows((i - 1) * blk + res, BLOCK), :]
            vp = pick_kv(jnp.concatenate([v_prev, vc_ref[q_rows, :]], axis=0)).astype(BF16)
            outs, lses = [], []
            for half in range(2):
                sel = lo_half if half == 0 else jnp.logical_not(lo_half)
                qm = jnp.where(sel, qp, 0.0).astype(BF16)
                s = _dot_nt(qm, kp)
                s = jnp.where(valid, s, MASK_VALUE)
                m = jnp.max(s, axis=-1, keepdims=True)
                if has_sink:
                    sk = sink_ref[2 * pair + half]
                    m = jnp.maximum(m, sk)
                pe = jnp.exp(s - m)
                den = jnp.sum(pe, axis=-1, keepdims=True)
                if has_sink:
                    den = den + jnp.exp(sk - m)
                outs.append(_dot(pe.astype(BF16), vp) / den)
                lses.append(m + jnp.log(den))
            o_ref[q_rows, :] = jnp.where(lo_half, outs[0], outs[1])
            if lse_ref is not None:
                lse_ref[q_rows, :] = jnp.where(lo_half, lses[0], lses[1])


def _band_attn(proj, tables, q_off, k_off, v_off, width, kv_width, dil, nq, max_dist, sinks=None):
    bsz, seq, _ = proj.shape
    blk = BLOCK * dil
    span = nq * blk
    gqa = kv_width != width
    assert kv_width == (LANES if gqa else width)
    n_pairs = width // LANES
    rows_per_pass = min(256, span)
    kv_col = (lambda off: (lambda p: off // LANES)) if gqa else (lambda off: (lambda p: off // LANES + p))
    cur = lambda col: (lambda b, j, p: (b, j, col(p)))
    prev = lambda col: (lambda b, j, p: (b, jnp.maximum(j * nq - 1, 0), col(p)))
    tcur = lambda b, j, p: (j, 0)
    tprev = lambda b, j, p: (jnp.maximum(j * nq - 1, 0), 0)
    in_specs = [pl.BlockSpec((None, span, LANES), cur(lambda p: q_off // LANES + p)),
                pl.BlockSpec((None, span, LANES), cur(kv_col(k_off))),
                pl.BlockSpec((None, blk, LANES), prev(kv_col(k_off))),
                pl.BlockSpec((None, span, LANES), cur(kv_col(v_off))),
                pl.BlockSpec((None, blk, LANES), prev(kv_col(v_off)))]
    in_specs += [pl.BlockSpec((span, LANES), tcur)] * 3 + [pl.BlockSpec((blk, LANES), tprev)] * 3
    args = [proj] * 5 + list(tables) + list(tables)
    has_sink = sinks is not None
    if has_sink:
        in_specs = [pl.BlockSpec(memory_space=pltpu.SMEM)] + in_specs
        args = [sinks] + args
    o_spec = pl.BlockSpec((None, span, LANES), lambda b, j, p: (b, j, p))
    o_shape = jax.ShapeDtypeStruct((bsz, seq, width), F32)
    out_shape, out_specs = (o_shape, o_spec) if has_sink else ((o_shape, o_shape), (o_spec, o_spec))
    return pl.pallas_call(
        functools.partial(_band_attn_kernel, dil=dil, nq=nq, max_dist=max_dist, has_sink=has_sink, gqa=gqa,
                          rows_per_pass=rows_per_pass),
        out_shape=out_shape,
        grid=(bsz, seq // span, n_pairs),
        in_specs=in_specs,
        out_specs=out_specs,
        scratch_shapes=[pltpu.VMEM((span, LANES), F32), pltpu.VMEM((span + blk, LANES), F32)],
        compiler_params=_cparams(("parallel", "arbitrary", "arbitrary")),
        name="swa_attn" if has_sink else f"dil_attn_{dil}",
    )(*args)


def _hgrn_kernel(q_ref, f_ref, i_ref, g_ref, lb_ref, ng_ref, ed_ref, o_ref, st_ref, b_sc, k_sc):
    L, C, D = HGRN_CHUNK, HGRN_SUB, HGRN_DIM

    @pl.when(pl.program_id(1) == 0)
    def _():
        st_ref[...] = jnp.zeros_like(st_ref)

    row = lax.broadcasted_iota(jnp.int32, (L, L), 0)
    col = lax.broadcasted_iota(jnp.int32, (L, L), 1)
    tri = (col <= row).astype(BF16)
    rowv = lax.broadcasted_iota(jnp.int32, (L, D), 0)
    same_block = lambda size: (row >> int(math.log2(size))) == (col >> int(math.log2(size)))
    diag_mask = same_block(C) & (col <= row)

    for hd in range(HGRN_HEADS):
        cs = slice(hd * D, (hd + 1) * D)
        x = f_ref[:, cs]
        lb = lb_ref[:, cs]
        log_sig = jnp.minimum(x, 0.0) - jnp.log1p(jnp.exp(-jnp.abs(x)))
        t0 = jnp.log(jnp.maximum(lb, LB_FLOOR))
        t1 = jnp.log1p(-lb) + log_sig
        log_f = jnp.maximum(t0, t1) + jnp.log1p(jnp.exp(-jnp.abs(t0 - t1)))
        kf = (1.0 - lb) * _sigmoid(-x)
        qx = q_ref[:, cs]
        qf = qx * _sigmoid(qx)
        v = i_ref[:, cs]
        vb = v.astype(BF16)

        hi, mid, lo = _split3(log_f)
        b = _dot(tri, hi) + _dot(tri, mid) + _dot(tri, lo)
        b_sc[...] = b
        k_sc[...] = kf

        scores = jnp.zeros((L, L), F32)
        m = C
        while m < L:
            ref_rows = [_row_bcast(b_sc, s0 + m - 1, 2 * m) for s0 in range(0, L, 2 * m)]
            ref = ref_rows[0] if len(ref_rows) == 1 else jnp.concatenate(ref_rows, axis=0)
            upper = ((rowv >> int(math.log2(m))) & 1) == 1
            w = jnp.exp(jnp.where(upper, b - ref, ref - b))
            qd = jnp.where(upper, qf * w, 0.0).astype(BF16)
            kd = jnp.where(upper, 0.0, kf * w).astype(BF16)
            scores = scores + jnp.where(same_block(2 * m), _dot_nt(qd, kd), 0.0)
            m *= 2

        xs = []
        for s in range(C):
            bs = jnp.concatenate([_row_bcast(b_sc, blk0 + s, C) for blk0 in range(0, L, C)], axis=0)
            ks = jnp.concatenate([_row_bcast(k_sc, blk0 + s, C) for blk0 in range(0, L, C)], axis=0)
            xs.append((qf * jnp.exp(jnp.minimum(b - bs, 0.0)) * ks).astype(BF16))
        diag = _dot(jnp.concatenate(xs, axis=1), ed_ref[...])
        scores = scores + jnp.where(diag_mask, diag, 0.0)

        st = st_ref[hd]
        o = _dot(scores.astype(BF16), vb) + _dot_nt((qf * jnp.exp(b)).astype(BF16), st.astype(BF16))

        b_last = _row_bcast(b_sc, L - 1, L)
        kd_end = (kf * jnp.exp(b_last - b)).astype(BF16)
        st_ref[hd] = st * jnp.exp(b_last) + _dot(v.T.astype(BF16), kd_end)

        ms = jnp.mean(o * o, axis=-1, keepdims=True)
        gx = g_ref[:, cs]
        o_ref[:, cs] = o * lax.rsqrt(ms + EPS) * ng_ref[...] * (gx * _sigmoid(gx))


def _diag_sum_matrix():
    c, d = HGRN_SUB, HGRN_DIM
    rows_s = np.arange(c * d) // d
    cols = np.arange(HGRN_CHUNK) % c
    return jnp.asarray((rows_s[:, None] == cols[None, :]).astype(np.float32), dtype=BF16)


def _hgrn(proj, lower_bound, norm_g):
    bsz, seq, _ = proj.shape
    L = HGRN_CHUNK
    base = OFF_HGRN // HGRN_WIDTH
    col = lambda c: (lambda b, j: (b, j, base + c))
    const = lambda b, j: (0, 0)
    return pl.pallas_call(
        _hgrn_kernel,
        out_shape=jax.ShapeDtypeStruct((bsz, seq, HGRN_WIDTH), F32),
        grid=(bsz, seq // L),
        in_specs=[pl.BlockSpec((None, L, HGRN_WIDTH), col(c)) for c in range(4)]
        + [pl.BlockSpec((1, HGRN_WIDTH), const), pl.BlockSpec((1, HGRN_DIM), const),
           pl.BlockSpec((HGRN_SUB * HGRN_DIM, L), const)],
        out_specs=pl.BlockSpec((None, L, HGRN_WIDTH), lambda b, j: (b, j, 0)),
        scratch_shapes=[pltpu.VMEM((HGRN_HEADS, HGRN_DIM, HGRN_DIM), F32),
                        pltpu.VMEM((L, HGRN_DIM), F32), pltpu.VMEM((L, HGRN_DIM), F32)],
        compiler_params=_cparams(("parallel", "arbitrary")),
        name="hgrn2",
    )(proj, proj, proj, proj, lower_bound.reshape(1, HGRN_WIDTH), norm_g.reshape(1, HGRN_DIM), _diag_sum_matrix())


def _s5_kernel(u0_ref, u1_ref, u2_ref, u3_ref, wb_ref, wc_ref, ar_ref, ai_ref, alr_ref, ali_ref,
               d_ref, gw_ref, gb_ref, o_ref, xs, carry, perm_sc):
    tm = S5_TILE
    lc = tm // S5_LANES
    half = S5_WIDTH
    u_refs = (u0_ref, u1_ref, u2_ref, u3_ref)

    @pl.when(pl.program_id(1) == 0)
    def _():
        carry[...] = jnp.zeros_like(carry)

    for c in range(S5_COLS):
        for tau in range(lc):
            perm_sc[c, tau * S5_LANES:(tau + 1) * S5_LANES, :] = u_refs[c][pl.ds(tau, S5_LANES, stride=lc), :]
        xs[:, c * 2 * half:(c + 1) * 2 * half] = _dot(perm_sc[c].astype(BF16), wb_ref[c])

    def cmul_add(ar, ai, xr, xi, br, bi):
        return ar * xr - ai * xi + br, ar * xi + ai * xr + bi

    for c in range(S5_COLS):
        re = slice(c * 2 * half, c * 2 * half + half)
        im = slice(c * 2 * half + half, (c + 1) * 2 * half)
        ar = jnp.broadcast_to(ar_ref[:, c * half:(c + 1) * half], (S5_LANES, half))
        ai = jnp.broadcast_to(ai_ref[:, c * half:(c + 1) * half], (S5_LANES, half))

        def step(tau, st, re=re, im=im, ar=ar, ai=ai):
            r0 = pl.multiple_of(tau * S5_LANES, S5_LANES)
            nr, ni = cmul_add(ar, ai, st[0], st[1], xs[pl.ds(r0, S5_LANES), re], xs[pl.ds(r0, S5_LANES), im])
            xs[pl.ds(r0, S5_LANES), re] = nr
            xs[pl.ds(r0, S5_LANES), im] = ni
            return nr, ni

        zero = jnp.zeros((S5_LANES, half), F32)
        lax.fori_loop(0, lc, step, (zero, zero), unroll=4)

    last = (lc - 1) * S5_LANES
    sub_id = lax.broadcasted_iota(jnp.int32, (S5_LANES, half), 0)
    for c in range(S5_COLS):
        re = slice(c * 2 * half, c * 2 * half + half)
        im = slice(c * 2 * half + half, (c + 1) * 2 * half)
        alr, ali = alr_ref[:, c * half:(c + 1) * half], ali_ref[:, c * half:(c + 1) * half]
        gr, gi = carry[0:1, re], carry[0:1, im]
        g_re = jnp.zeros((S5_LANES, half), F32)
        g_im = jnp.zeros((S5_LANES, half), F32)
        for sub in range(S5_LANES):
            g_re = jnp.where(sub_id == sub, gr, g_re)
            g_im = jnp.where(sub_id == sub, gi, g_im)
            gr, gi = cmul_add(alr, ali, gr, gi, xs[pl.ds(last + sub, 1), re], xs[pl.ds(last + sub, 1), im])
        carry[0:1, re] = gr
        carry[0:1, im] = gi
        ar = jnp.broadcast_to(ar_ref[:, c * half:(c + 1) * half], (S5_LANES, half))
        ai = jnp.broadcast_to(ai_ref[:, c * half:(c + 1) * half], (S5_LANES, half))

        def fix(tau, st, re=re, im=im, ar=ar, ai=ai):
            r0 = pl.multiple_of(tau * S5_LANES, S5_LANES)
            cr, ci = cmul_add(ar, ai, st[0], st[1], 0.0, 0.0)
            xs[pl.ds(r0, S5_LANES), re] = xs[pl.ds(r0, S5_LANES), re] + cr
            xs[pl.ds(r0, S5_LANES), im] = xs[pl.ds(r0, S5_LANES), im] + ci
            return cr, ci

        lax.fori_loop(0, lc, fix, (g_re, g_im), unroll=4)

    ys = []
    for c in range(S5_COLS):
        perm_sc[c] = _dot(xs[:, c * 2 * half:(c + 1) * 2 * half].astype(BF16), wc_ref[c])
        y = jnp.concatenate([perm_sc[c, pl.ds(sub, lc, stride=S5_LANES), :] for sub in range(S5_LANES)], axis=0)
        ys.append(y + d_ref[:, c * LANES:(c + 1) * LANES] * u_refs[c][...])
    y = jnp.concatenate(ys, axis=1)
    y = y * (0.5 * (1.0 + jnp.tanh(math.sqrt(2.0 / math.pi) * (y + 0.044715 * (y * y * y)))))
    o_ref[...] = y * _sigmoid(_dot(y.astype(BF16), gw_ref[...]) + gb_ref[...])


def _s5_params(a_re, a_im, b_re, b_im, c_re, c_im, log_step):
    a = lax.complex(a_re.astype(F32), a_im.astype(F32))
    dt = jnp.exp(log_step.astype(F32))[:, None]
    a_bar = jnp.exp(dt * a)
    b_bar = ((a_bar - 1.0) / a)[..., None] * lax.complex(b_re.astype(F32), b_im.astype(F32))
    a_l = a_bar ** (S5_TILE // S5_LANES)
    gpc = S5_GROUPS // S5_COLS
    eye = jnp.eye(gpc, dtype=F32)

    def in_w(t):
        t = t.reshape(S5_COLS, gpc, S5_STATE, S5_GROUP)
        return jnp.einsum('cgpi,gh->cgihp', t, eye).reshape(S5_COLS, gpc * S5_GROUP, gpc * S5_STATE)

    def out_w(t):
        t = t.reshape(S5_COLS, gpc, S5_GROUP, S5_STATE)
        return jnp.einsum('cgip,gh->cgphi', t, eye).reshape(S5_COLS, gpc * S5_STATE, gpc * S5_GROUP)

    wb = jnp.concatenate([in_w(jnp.real(b_bar)), in_w(jnp.imag(b_bar))], axis=2).astype(BF16)
    wc = jnp.concatenate([out_w(c_re.astype(F32)), -out_w(c_im.astype(F32))], axis=1).astype(BF16)
    flat = lambda t: t.reshape(1, S5_GROUPS * S5_STATE)
    return wb, wc, flat(jnp.real(a_bar)), flat(jnp.imag(a_bar)), flat(jnp.real(a_l)), flat(jnp.imag(a_l))


def _s5(proj, params, d, glu_w, glu_b):
    bsz, seq, _ = proj.shape
    tm = S5_TILE
    wb, wc, ar, ai, alr, ali = params
    nstate = S5_GROUPS * S5_STATE
    base = OFF_S5 // LANES
    in_specs = [pl.BlockSpec((None, tm, LANES), (lambda c: (lambda b, j: (b, j, base + c)))(c)) for c in range(S5_COLS)]
    in_specs += [_resident((S5_COLS, LANES, 2 * S5_WIDTH)), _resident((S5_COLS, 2 * S5_WIDTH, LANES))]
    in_specs += [_resident((1, nstate))] * 4
    in_specs += [_resident((1, S5_WIDTH)), _resident((S5_WIDTH, S5_WIDTH)), _resident((1, S5_WIDTH))]
    return pl.pallas_call(
        _s5_kernel,
        out_shape=jax.ShapeDtypeStruct((bsz, seq, S5_WIDTH), F32),
        grid=(bsz, seq // tm),
        in_specs=in_specs,
        out_specs=pl.BlockSpec((None, tm, S5_WIDTH), lambda b, j: (b, j, 0)),
        scratch_shapes=[pltpu.VMEM((tm, 2 * nstate), F32), pltpu.VMEM((S5_LANES, 2 * nstate), F32),
                        pltpu.VMEM((S5_COLS, tm, LANES), F32)],
        compiler_params=_cparams(("parallel", "arbitrary")),
        name="s5_ssm",
    )(proj, proj, proj, proj, wb, wc, ar, ai, alr, ali, d.reshape(1, S5_WIDTH), glu_w.astype(BF16),
      glu_b.reshape(1, S5_WIDTH))


def _merge_kernel(h_ref, g_ref, ya_ref, yb_ref, yc_ref, o1_ref, o2_ref, o3_ref, l1_ref, l2_ref, l3_ref,
                  wg_ref, wa_ref, wb_ref, wc_ref, wd_ref, wo_ref, out_ref):
    h = h_ref[...]
    ms = jnp.mean(h * h, axis=-1, keepdims=True)
    hn = (h * lax.rsqrt(ms + EPS) * g_ref[...]).astype(BF16)
    l1, l2, l3 = l1_ref[...], l2_ref[...], l3_ref[...]
    mx = jnp.maximum(jnp.maximum(l1, l2), l3)
    e1, e2, e3 = jnp.exp(l1 - mx), jnp.exp(l2 - mx), jnp.exp(l3 - mx)
    tot = e1 + e2 + e3
    yd = (e1 / tot) * o1_ref[...] + (e2 / tot) * o2_ref[...] + (e3 / tot) * o3_ref[...]

    def branch(c, y, w_ref):
        gate = _sigmoid(_dot(hn, wg_ref[:, c * D_MODEL:(c + 1) * D_MODEL]))
        return gate * _dot(y.astype(BF16), w_ref[...])

    merged = (branch(0, ya_ref[...], wa_ref) + branch(1, yb_ref[...], wb_ref) + branch(2, yc_ref[...], wc_ref)
              + branch(3, yd, wd_ref))
    out_ref[...] = h + _dot(merged.astype(BF16), wo_ref[...])


def _merge(h2d, g, ya, yb, yc, dil, wg, wa, wb, wc, wd, wo, tm=256):
    t = h2d.shape[0]
    row = lambda i: (i, 0)
    in_specs = [pl.BlockSpec((tm, D_MODEL), row), _resident((1, D_MODEL))]
    in_specs += [pl.BlockSpec((tm, y.shape[1]), row) for y in (ya, yb, yc)]
    in_specs += [pl.BlockSpec((tm, DIL_OUT), row)] * 6
    in_specs += [_resident(w.shape) for w in (wg, wa, wb, wc, wd, wo)]
    (o1, s1), (o2, s2), (o3, s3) = dil
    return pl.pallas_call(
        _merge_kernel,
        out_shape=jax.ShapeDtypeStruct((t, D_MODEL), F32),
        grid=(t // tm,),
        in_specs=in_specs,
        out_specs=pl.BlockSpec((tm, D_MODEL), row),
        compiler_params=_cparams(("parallel",)),
        name="branch_merge",
    )(h2d, g.reshape(1, D_MODEL), ya, yb, yc, o1, o2, o3, s1, s2, s3, wg, wa, wb, wc, wd, wo)


def _ffn_kernel(h_ref, hp_ref, g_ref, wu_ref, cw_ref, cb_ref, wd_ref, fg_ref, o_ref, hn_sc, z_sc, *, final_norm):
    i = pl.program_id(1)
    tf, halo = FFN_TF, FFN_HALO

    def norm(x):
        ms = jnp.mean(x * x, axis=-1, keepdims=True)
        return x * lax.rsqrt(ms + EPS) * g_ref[...]

    hn_sc[halo:, :] = norm(h_ref[...]).astype(BF16)
    hn_sc[:halo, :] = jnp.where(i == 0, 0.0, norm(hp_ref[...])).astype(BF16)
    hn = hn_sc[...]

    def conv(cols):
        u = _dot(hn, wu_ref[:, cols])
        u1 = pltpu.roll(u, 1, 0)
        u2 = pltpu.roll(u, 2, 0)
        return (cw_ref[0:1, cols] * u2[halo:, :] + cw_ref[1:2, cols] * u1[halo:, :] + cw_ref[2:3, cols] * u[halo:, :]
                + cb_ref[:, cols])

    for c in range(D_FF // tf):
        a = conv(slice(c * tf, (c + 1) * tf))
        b = conv(slice(D_FF + c * tf, D_FF + (c + 1) * tf))
        z_sc[:, c * tf:(c + 1) * tf] = ((a * _sigmoid(a)) * b).astype(BF16)

    y = h_ref[...] + _dot(z_sc[...], wd_ref[...])
    if final_norm:
        ms = jnp.mean(y * y, axis=-1, keepdims=True)
        y = y * lax.rsqrt(ms + EPS) * fg_ref[...]
    o_ref[...] = y


def _ffn(h, g, w_up, conv_w, conv_b, w_down, final_g, final_norm):
    bsz, seq, _ = h.shape
    tm, halo = FFN_TM, FFN_HALO
    return pl.pallas_call(
        functools.partial(_ffn_kernel, final_norm=final_norm),
        out_shape=jax.ShapeDtypeStruct((bsz, seq, D_MODEL), F32),
        grid=(bsz, seq // tm),
        in_specs=[pl.BlockSpec((None, tm, D_MODEL), lambda b, i: (b, i, 0)),
                  pl.BlockSpec((None, halo, D_MODEL), lambda b, i: (b, jnp.maximum(i * (tm // halo) - 1, 0), 0)),
                  _resident((1, D_MODEL)), _resident((D_MODEL, 2 * D_FF)), _resident((CONV_WIDTH, 2 * D_FF)),
                  _resident((1, 2 * D_FF)), _resident((D_FF, D_MODEL)), _resident((1, D_MODEL))],
        out_specs=pl.BlockSpec((None, tm, D_MODEL), lambda b, i: (b, i, 0)),
        scratch_shapes=[pltpu.VMEM((tm + halo, D_MODEL), BF16), pltpu.VMEM((tm, D_FF), BF16)],
        compiler_params=_cparams(("parallel", "arbitrary")),
        name="conv_ffn",
    )(h, h, g.reshape(1, D_MODEL), w_up, conv_w, conv_b.reshape(1, -1), w_down, final_g.reshape(1, D_MODEL))


def kernel(x, norm1_g, w_in, hgrn_lb_logits, hgrn_norm_g, attn_sinks, s5_a_re, s5_a_im, s5_b_re, s5_b_im, s5_c_re, s5_c_im, s5_d, s5_log_step, s5_glu_w, s5_glu_b, w_branch_a, w_branch_b, w_branch_c, w_branch_d, w_out, norm2_g, ffn_w_up, ffn_conv_w, ffn_conv_b, ffn_w_down, final_norm_g):
    bsz, seq, _ = x.shape
    tables = _rope_tables(seq)
    p = jax.nn.softmax(hgrn_lb_logits.astype(F32), axis=0)
    lower_bounds = jnp.cumsum(p, axis=0) - p[0]
    perm = _column_permutation()

    h = x
    for l in range(DEPTH):
        w_in_l = jnp.take(w_in[l], perm, axis=1).astype(BF16)
        proj2d = _in_proj(h.reshape(bsz * seq, D_MODEL), norm1_g[l], w_in_l)
        proj = proj2d.reshape(bsz, seq, D_IN)

        y_a = _hgrn(proj, lower_bounds[l], hgrn_norm_g[l])
        y_b = _band_attn(proj, tables, OFF_SWA_Q, OFF_SWA_K, OFF_SWA_V, SWA_Q_WIDTH, SWA_KV_WIDTH, dil=1, nq=8,
                         max_dist=SWA_WINDOW - 1, sinks=attn_sinks[l].astype(F32))
        y_c = _s5(proj, _s5_params(s5_a_re[l], s5_a_im[l], s5_b_re[l], s5_b_im[l], s5_c_re[l], s5_c_im[l],
                                   s5_log_step[l]), s5_d[l], s5_glu_w[l], s5_glu_b[l])
        dil = []
        for g, (window, dilation) in enumerate(DIL_PAIRS):
            o, lse = _band_attn(proj, tables, OFF_DIL_Q + g * DIL_OUT, OFF_DIL_K + g * DIL_OUT,
                                OFF_DIL_V + g * DIL_OUT, DIL_OUT, DIL_OUT, dil=dilation, nq=DIL_NQ[g],
                                max_dist=window // dilation)
            dil.append((o.reshape(bsz * seq, DIL_OUT), lse.reshape(bsz * seq, DIL_OUT)))

        flat = lambda t: t.reshape(bsz * seq, t.shape[-1])
        h2d = _merge(h.reshape(bsz * seq, D_MODEL), norm1_g[l], flat(y_a), flat(y_b), flat(y_c), dil,
                     w_in[l][:, REF_GATE_OFF:].astype(BF16), w_branch_a[l].astype(BF16), w_branch_b[l].astype(BF16),
                     w_branch_c[l].astype(BF16), w_branch_d[l].astype(BF16), w_out[l].astype(BF16))
        h = _ffn(h2d.reshape(bsz, seq, D_MODEL), norm2_g[l], ffn_w_up[l].astype(BF16), ffn_conv_w[l], ffn_conv_b[l],
                 ffn_w_down[l].astype(BF16), final_norm_g, final_norm=(l == DEPTH - 1))
    return h
```

```python
import functools
import math

import numpy as np
import jax
import jax.numpy as jnp
from jax import lax
from jax.experimental import pallas as pl
from jax.experimental.pallas import tpu as pltpu

F32 = jnp.float32
BF16 = jnp.bfloat16

D_MODEL = 1024
DEPTH = 2
HEAD_DIM = 64
BLOCK = 128
ROPE_THETA = 500000.0
ROT_DIM = HEAD_DIM // 4
EPS = 1e-6
MASK_VALUE = -1e30
LB_FLOOR = 1e-30
N_BRANCH = 4

HGRN_HEADS = 4
HGRN_DIM = 128
HGRN_WIDTH = HGRN_HEADS * HGRN_DIM
HGRN_CHUNK = 128
HGRN_CHUNKS_PER_STEP = 2
HGRN_SUB = 8

SWA_Q_HEADS = 8
SWA_KV_HEADS = 2
SWA_WINDOW = 128
SWA_Q_WIDTH = SWA_Q_HEADS * HEAD_DIM
SWA_KV_WIDTH = SWA_KV_HEADS * HEAD_DIM

S5_WIDTH = 512
S5_GROUP = 16
S5_GROUPS = S5_WIDTH // S5_GROUP
S5_STATE = 64
S5_COLS = 4
S5_TILE = 512
S5_LANES = 8

DIL_PAIRS = ((128, 1), (512, 4), (2048, 16))
DIL_HEADS_PER_GROUP = 4
DIL_HEADS = DIL_HEADS_PER_GROUP * len(DIL_PAIRS)
DIL_OUT = DIL_HEADS_PER_GROUP * HEAD_DIM
DIL_NQ = (8, 2, 1)

D_FF = 2816
CONV_WIDTH = 3
FFN_TF = 256
FFN_TM = 512
FFN_HALO = 16

LANES = 128

OFF_HGRN = 0
OFF_SWA_Q = OFF_HGRN + 4 * HGRN_WIDTH
OFF_S5 = OFF_SWA_Q + SWA_Q_WIDTH
OFF_DIL_Q = OFF_S5 + S5_WIDTH
OFF_DIL_K = OFF_DIL_Q + DIL_HEADS * HEAD_DIM
OFF_DIL_V = OFF_DIL_K + DIL_HEADS * HEAD_DIM
OFF_SWA_K = OFF_DIL_V + DIL_HEADS * HEAD_DIM
OFF_SWA_V = OFF_SWA_K + SWA_KV_WIDTH
D_IN = OFF_SWA_V + SWA_KV_WIDTH

_REF_SIZES = (HGRN_WIDTH,) * 4 + (SWA_Q_WIDTH, SWA_KV_WIDTH, SWA_KV_WIDTH, S5_WIDTH) + (DIL_HEADS * HEAD_DIM,) * 3 + (N_BRANCH * D_MODEL,)
_REF_OFFS = np.concatenate([[0], np.cumsum(_REF_SIZES)])[:-1]
_NEW_OFFS = (OFF_HGRN, OFF_HGRN + 512, OFF_HGRN + 1024, OFF_HGRN + 1536, OFF_SWA_Q, OFF_SWA_K, OFF_SWA_V, OFF_S5,
             OFF_DIL_Q, OFF_DIL_K, OFF_DIL_V)
REF_GATE_OFF = int(_REF_OFFS[-1])

VMEM_LIMIT = 56 * 1024 * 1024


def _permute_columns(w):
    order = np.argsort(np.asarray(_NEW_OFFS))
    return jnp.concatenate([w[..., int(_REF_OFFS[k]):int(_REF_OFFS[k]) + _REF_SIZES[k]] for k in order], axis=-1)


def _cparams(sem):
    return pltpu.CompilerParams(dimension_semantics=sem, vmem_limit_bytes=VMEM_LIMIT)


def _resident(shape):
    return pl.BlockSpec(shape, lambda *_: (0,) * len(shape), pipeline_mode=pl.Buffered(1))


def _sigmoid(x):
    return 1.0 / (1.0 + jnp.exp(-x))


def _split3(x):
    hi = x.astype(BF16)
    r1 = x - hi.astype(F32)
    mid = r1.astype(BF16)
    lo = (r1 - mid.astype(F32)).astype(BF16)
    return hi, mid, lo


def _row_bcast(ref, r, n):
    return jnp.broadcast_to(ref[pl.ds(r, 1), :], (n, ref.shape[1]))


def _dot(a, b):
    return jnp.dot(a, b, preferred_element_type=F32)


def _dot_nt(a, b):
    return lax.dot_general(a, b, (((1,), (1,)), ((), ())), preferred_element_type=F32)


_ROPE_COLS = ((OFF_SWA_Q, SWA_Q_WIDTH, HEAD_DIM ** -0.5), (OFF_DIL_Q, DIL_HEADS * HEAD_DIM, HEAD_DIM ** -0.5),
              (OFF_DIL_K, DIL_HEADS * HEAD_DIM, 1.0), (OFF_SWA_K, SWA_KV_WIDTH, 1.0))


def _in_proj_kernel(x_ref, g_ref, w_ref, cos_ref, sup_ref, sdn_ref, o_ref, hn_ref, *, tn):
    j = pl.program_id(1)

    @pl.when(j == 0)
    def _():
        x = x_ref[...]
        ms = jnp.mean(x * x, axis=-1, keepdims=True)
        hn_ref[...] = (x * lax.rsqrt(ms + EPS) * g_ref[...]).astype(BF16)

    o_ref[...] = _dot(hn_ref[...], w_ref[...])

    for jt in range(D_IN // tn):
        spans = [(max(a, jt * tn) - jt * tn, min(a + w, (jt + 1) * tn) - jt * tn, sc) for a, w, sc in _ROPE_COLS]
        spans = [s for s in spans if s[1] > s[0]]

        @pl.when(j == jt)
        def _(spans=spans):
            for lo, hi, scale in spans:
                for c0 in range(lo, hi, LANES):
                    r = _rope(o_ref[:, c0:c0 + LANES], cos_ref[...], sup_ref[...], sdn_ref[...])
                    o_ref[:, c0:c0 + LANES] = r if scale == 1.0 else r * scale


def _in_proj(h2d, g, w_bf16, tables, seq, tm=1024, tn=D_IN // 2):
    t = h2d.shape[0]
    pos = lambda i, j: (i % (seq // tm), 0)
    return pl.pallas_call(
        functools.partial(_in_proj_kernel, tn=tn),
        out_shape=jax.ShapeDtypeStruct((t, D_IN), F32),
        grid=(t // tm, D_IN // tn),
        in_specs=[pl.BlockSpec((tm, D_MODEL), lambda i, j: (i, 0)),
                  pl.BlockSpec((1, D_MODEL), lambda i, j: (0, 0)),
                  pl.BlockSpec((D_MODEL, tn), lambda i, j: (0, j))] + [pl.BlockSpec((tm, LANES), pos)] * 3,
        out_specs=pl.BlockSpec((tm, tn), lambda i, j: (i, j)),
        scratch_shapes=[pltpu.VMEM((tm, D_MODEL), BF16)],
        compiler_params=_cparams(("parallel", "arbitrary")),
        name="in_proj",
    )(h2d, g.reshape(1, D_MODEL), w_bf16, *tables)


def _rope_tables(seq):
    inv_freq = ROPE_THETA ** (-jnp.arange(0, ROT_DIM, 2, dtype=F32) / ROT_DIM)
    ang = jnp.arange(seq, dtype=F32)[:, None] * inv_freq[None, :]
    cos, sin = jnp.cos(ang), jnp.sin(ang)
    half = ROT_DIM // 2
    rest = HEAD_DIM - ROT_DIM
    one, zero = jnp.ones((seq, rest), F32), jnp.zeros((seq, rest), F32)
    zh = jnp.zeros((seq, half), F32)
    cos_h = jnp.concatenate([cos, cos, one], axis=1)
    sup_h = jnp.concatenate([-sin, zh, zero], axis=1)
    sdn_h = jnp.concatenate([zh, sin, zero], axis=1)
    reps = LANES // HEAD_DIM
    return tuple(jnp.tile(t, (1, reps)) for t in (cos_h, sup_h, sdn_h))


def _rope(x, cos, sup, sdn):
    half = ROT_DIM // 2
    return x * cos + pltpu.roll(x, LANES - half, 1) * sup + pltpu.roll(x, half, 1) * sdn


def _band_attn_kernel(*refs, dil, nq, max_dist, has_sink, gqa, pairs):
    if has_sink:
        sink_ref, refs = refs[0], refs[1:]
    q_ref, kc_ref, kp_ref, vc_ref, vp_ref, o_ref = refs[:6]
    refs = refs[6:]
    lse_ref = None
    if not has_sink:
        lse_ref, refs = refs[0], refs[1:]
    j = pl.program_id(1)
    pair0 = pl.program_id(2) * pairs
    blk = BLOCK * dil
    span = nq * blk

    lane = lax.broadcasted_iota(jnp.int32, (1, LANES), 1)
    lo_half = lane < HEAD_DIM

    if gqa:
        k_sc, v_sc = refs
        for sc, prev_ref, cur_ref in ((k_sc, kp_ref, kc_ref), (v_sc, vp_ref, vc_ref)):
            for src, base, n in ((prev_ref, 0, blk), (cur_ref, blk, span)):
                for r0 in range(0, n, min(n, 2 * BLOCK)):
                    rs = slice(r0, r0 + min(n, 2 * BLOCK))
                    t = src[rs, :]
                    sw = pltpu.roll(t, HEAD_DIM, 1)
                    sc[0, base + r0:base + rs.stop, :] = jnp.where(lo_half, t, sw)
                    sc[1, base + r0:base + rs.stop, :] = jnp.where(lo_half, sw, t)

    per_kv = SWA_Q_HEADS // (2 * SWA_KV_HEADS) if gqa else 1
    stack = 2 * per_kv
    qi = lax.broadcasted_iota(jnp.int32, (stack * BLOCK, 2 * BLOCK), 0) & (BLOCK - 1)
    ki = lax.broadcasted_iota(jnp.int32, (stack * BLOCK, 2 * BLOCK), 1)
    dist = qi + BLOCK - ki
    band = (dist >= 0) & (dist <= max_dist)
    head_of_row = lax.broadcasted_iota(jnp.int32, (stack * BLOCK, 1), 0) >> int(math.log2(BLOCK))

    def rows(start, size):
        return pl.ds(start, size) if dil == 1 else pl.ds(start, size, stride=dil)

    def load_kv(cur_ref, prev_ref, sc, i, res, p):
        if gqa:
            return sc[p // (SWA_Q_HEADS // (2 * SWA_KV_HEADS)), i * blk:i * blk + 2 * BLOCK, :]
        cols = slice(p * LANES, (p + 1) * LANES)
        prev = prev_ref[rows(res, BLOCK), cols] if i == 0 else cur_ref[rows((i - 1) * blk + res, BLOCK), cols]
        return jnp.concatenate([prev, cur_ref[rows(i * blk + res, BLOCK), cols]], axis=0)

    for res in range(dil):
        for i in range(nq):
            valid = band & (ki >= jnp.where(j == 0, BLOCK, 0)) if i == 0 else band
            q_rows = rows(i * blk + res, BLOCK)
            for p0 in range(0, pairs, per_kv):
                kp = load_kv(kc_ref, kp_ref, k_sc if gqa else None, i, res, p0).astype(BF16)
                vp = load_kv(vc_ref, vp_ref, v_sc if gqa else None, i, res, p0).astype(BF16)
                qs = []
                for p in range(p0, p0 + per_kv):
                    qp = q_ref[q_rows, p * LANES:(p + 1) * LANES]
                    qs += [jnp.where(lo_half, qp, 0.0), jnp.where(lo_half, 0.0, qp)]
                s = jnp.where(valid, _dot_nt(jnp.concatenate(qs, axis=0).astype(BF16), kp), MASK_VALUE)
                m = jnp.max(s, axis=-1, keepdims=True)
                if has_sink:
                    sk = sink_ref[2 * (pair0 + p0)]
                    for hh in range(1, stack):
                        sk = jnp.where(head_of_row == hh, sink_ref[2 * (pair0 + p0) + hh], sk)
                    m = jnp.maximum(m, sk)
                pe = jnp.exp(s - m)
                den = jnp.sum(pe, axis=-1, keepdims=True)
                if has_sink:
                    den = den + jnp.exp(sk - m)
                o = _dot(pe.astype(BF16), vp) / den
                lse = m + jnp.log(den)
                for p in range(p0, p0 + per_kv):
                    r0 = 2 * (p - p0) * BLOCK
                    cols = slice(p * LANES, (p + 1) * LANES)
                    o_ref[q_rows, cols] = jnp.where(lo_half, o[r0:r0 + BLOCK], o[r0 + BLOCK:r0 + 2 * BLOCK])
                    if lse_ref is not None:
                        lse_ref[q_rows, cols] = jnp.where(lo_half, lse[r0:r0 + BLOCK], lse[r0 + BLOCK:r0 + 2 * BLOCK])


def _band_attn(proj, q_off, k_off, v_off, width, kv_width, dil, nq, max_dist, sinks=None):
    bsz, seq, _ = proj.shape
    blk = BLOCK * dil
    span = nq * blk
    gqa = kv_width != width
    n_pairs = width // LANES
    pairs = n_pairs if dil == 1 else 1
    assert not gqa or (dil == 1 and kv_width == LANES)
    qw = pairs * LANES
    kw = kv_width if gqa else qw
    cur = lambda off, w: (lambda b, j, p: (b, j, off // w + p))
    prev = lambda off, w: (lambda b, j, p: (b, jnp.maximum(j * nq - 1, 0), off // w + p))
    in_specs = [pl.BlockSpec((None, span, qw), cur(q_off, qw)),
                pl.BlockSpec((None, span, kw), cur(k_off, kw)), pl.BlockSpec((None, blk, kw), prev(k_off, kw)),
                pl.BlockSpec((None, span, kw), cur(v_off, kw)), pl.BlockSpec((None, blk, kw), prev(v_off, kw))]
    args = [proj] * 5
    has_sink = sinks is not None
    if has_sink:
        in_specs = [pl.BlockSpec(memory_space=pltpu.SMEM)] + in_specs
        args = [sinks] + args
    o_spec = pl.BlockSpec((None, span, qw), lambda b, j, p: (b, j, p))
    o_shape = jax.ShapeDtypeStruct((bsz, seq, width), F32)
    out_shape, out_specs = (o_shape, o_spec) if has_sink else ((o_shape, o_shape), (o_spec, o_spec))
    return pl.pallas_call(
        functools.partial(_band_attn_kernel, dil=dil, nq=nq, max_dist=max_dist, has_sink=has_sink, gqa=gqa, pairs=pairs),
        out_shape=out_shape,
        grid=(bsz, seq // span, n_pairs // pairs),
        in_specs=in_specs,
        out_specs=out_specs,
        scratch_shapes=[pltpu.VMEM((SWA_KV_HEADS, span + blk, LANES), F32)] * 2 if gqa else [],
        compiler_params=_cparams(("parallel", "arbitrary", "arbitrary")),
        name="swa_attn" if has_sink else f"dil_attn_{dil}",
    )(*args)


def _hgrn_kernel(q_ref, f_ref, i_ref, g_ref, lb_ref, ng_ref, ed_ref, o_ref, st_ref, b_all, k_all):
    L, C, D = HGRN_CHUNK, HGRN_SUB, HGRN_DIM

    @pl.when(pl.program_id(1) == 0)
    def _():
        st_ref[...] = jnp.zeros_like(st_ref)

    row = lax.broadcasted_iota(jnp.int32, (L, L), 0)
    col = lax.broadcasted_iota(jnp.int32, (L, L), 1)
    tri = (col <= row).astype(BF16)
    rowv = lax.broadcasted_iota(jnp.int32, (L, D), 0)
    same_block = lambda size: (row >> int(math.log2(size))) == (col >> int(math.log2(size)))
    diag_mask = same_block(C) & (col <= row)

    for ck, hd in [(ck, hd) for ck in range(HGRN_CHUNKS_PER_STEP) for hd in range(HGRN_HEADS)]:
        cs = slice(hd * D, (hd + 1) * D)
        rs = slice(ck * L, (ck + 1) * L)
        b_sc, k_sc = b_all.at[ck * HGRN_HEADS + hd], k_all.at[ck * HGRN_HEADS + hd]
        x = f_ref[rs, cs]
        lb = lb_ref[:, cs]
        e = jnp.exp(-jnp.abs(x))
        log_sig = jnp.minimum(x, 0.0) - jnp.log(1.0 + e)
        t0 = jnp.log(jnp.maximum(lb, LB_FLOOR))
        t1 = jnp.log1p(-lb) + log_sig
        log_f = jnp.maximum(t0, t1) + jnp.log(1.0 + jnp.exp(-jnp.abs(t0 - t1)))
        kf = (1.0 - lb) * (jnp.where(x > 0.0, e, 1.0) / (1.0 + e))
        qx = q_ref[rs, cs]
        qf = qx * _sigmoid(qx)
        v = i_ref[rs, cs]
        vb = v.astype(BF16)

        hi, mid, lo = _split3(log_f)
        b = _dot(tri, hi) + _dot(tri, mid) + _dot(tri, lo)
        b_sc[...] = b
        k_sc[...] = kf

        scores = jnp.zeros((L, L), F32)
        m = C
        while m < L:
            ref_rows = [_row_bcast(b_sc, s0 + m - 1, 2 * m) for s0 in range(0, L, 2 * m)]
            ref = ref_rows[0] if len(ref_rows) == 1 else jnp.concatenate(ref_rows, axis=0)
            upper = ((rowv >> int(math.log2(m))) & 1) == 1
            w = jnp.exp(jnp.where(upper, b - ref, ref - b))
            qd = jnp.where(upper, qf * w, 0.0).astype(BF16)
            kd = jnp.where(upper, 0.0, kf * w).astype(BF16)
            scores = scores + jnp.where(same_block(2 * m), _dot_nt(qd, kd), 0.0)
            m *= 2

        xs = []
        for s in range(C):
            bs = jnp.concatenate([_row_bcast(b_sc, blk0 + s, C) for blk0 in range(0, L, C)], axis=0)
            ks = jnp.concatenate([_row_bcast(k_sc, blk0 + s, C) for blk0 in range(0, L, C)], axis=0)
            xs.append((qf * jnp.exp(jnp.minimum(b - bs, 0.0)) * ks).astype(BF16))
        diag = _dot(jnp.concatenate(xs, axis=1), ed_ref[...])
        scores = scores + jnp.where(diag_mask, diag, 0.0)

        st = st_ref[hd]
        o = _dot(scores.astype(BF16), vb) + _dot_nt((qf * jnp.exp(b)).astype(BF16), st.astype(BF16))

        b_last = _row_bcast(b_sc, L - 1, L)
        kd_end = (kf * jnp.exp(b_last - b)).astype(BF16)
        st_ref[hd] = st * jnp.exp(b_last) + _dot(v.T.astype(BF16), kd_end)

        ms = jnp.mean(o * o, axis=-1, keepdims=True)
        gx = g_ref[rs, cs]
        o_ref[rs, cs] = o * lax.rsqrt(ms + EPS) * ng_ref[...] * (gx * _sigmoid(gx))


def _diag_sum_matrix():
    c, d = HGRN_SUB, HGRN_DIM
    rows_s = np.arange(c * d) // d
    cols = np.arange(HGRN_CHUNK) % c
    return jnp.asarray((rows_s[:, None] == cols[None, :]).astype(np.float32), dtype=BF16)


def _hgrn(proj, lower_bound, norm_g):
    bsz, seq, _ = proj.shape
    L = HGRN_CHUNK
    rows = HGRN_CHUNKS_PER_STEP * L
    base = OFF_HGRN // HGRN_WIDTH
    col = lambda c: (lambda b, j: (b, j, base + c))
    per_chain = pltpu.VMEM((HGRN_CHUNKS_PER_STEP * HGRN_HEADS, L, HGRN_DIM), F32)
    return pl.pallas_call(
        _hgrn_kernel,
        out_shape=jax.ShapeDtypeStruct((bsz, seq, HGRN_WIDTH), F32),
        grid=(bsz, seq // rows),
        in_specs=[pl.BlockSpec((None, rows, HGRN_WIDTH), col(c)) for c in range(4)]
        + [_resident((1, HGRN_WIDTH)), _resident((1, HGRN_DIM)), _resident((HGRN_SUB * HGRN_DIM, L))],
        out_specs=pl.BlockSpec((None, rows, HGRN_WIDTH), lambda b, j: (b, j, 0)),
        scratch_shapes=[pltpu.VMEM((HGRN_HEADS, HGRN_DIM, HGRN_DIM), F32), per_chain, per_chain],
        compiler_params=_cparams(("parallel", "arbitrary")),
        name="hgrn2",
    )(proj, proj, proj, proj, lower_bound.reshape(1, HGRN_WIDTH), norm_g.reshape(1, HGRN_DIM), _diag_sum_matrix())


def _s5_kernel(u0_ref, u1_ref, u2_ref, u3_ref, wb_ref, wc_ref, ar_ref, ai_ref, alr_ref, ali_ref,
               d_ref, gw_ref, gb_ref, o_ref, xs, carry, perm_sc):
    tm = S5_TILE
    lc = tm // S5_LANES
    half = S5_WIDTH
    u_refs = (u0_ref, u1_ref, u2_ref, u3_ref)

    @pl.when(pl.program_id(1) == 0)
    def _():
        carry[...] = jnp.zeros_like(carry)

    for c in range(S5_COLS):
        for tau in range(lc):
            perm_sc[c, tau * S5_LANES:(tau + 1) * S5_LANES, :] = u_refs[c][pl.ds(tau, S5_LANES, stride=lc), :]
        xs[:, c * 2 * half:(c + 1) * 2 * half] = _dot(perm_sc[c].astype(BF16), wb_ref[c])

    def cmul_add(ar, ai, xr, xi, br, bi):
        return ar * xr - ai * xi + br, ar * xi + ai * xr + bi

    for c in range(S5_COLS):
        re = slice(c * 2 * half, c * 2 * half + half)
        im = slice(c * 2 * half + half, (c + 1) * 2 * half)
        ar = jnp.broadcast_to(ar_ref[:, c * half:(c + 1) * half], (S5_LANES, half))
        ai = jnp.broadcast_to(ai_ref[:, c * half:(c + 1) * half], (S5_LANES, half))

        def step(tau, st, re=re, im=im, ar=ar, ai=ai):
            r0 = pl.multiple_of(tau * S5_LANES, S5_LANES)
            nr, ni = cmul_add(ar, ai, st[0], st[1], xs[pl.ds(r0, S5_LANES), re], xs[pl.ds(r0, S5_LANES), im])
            xs[pl.ds(r0, S5_LANES), re] = nr
            xs[pl.ds(r0, S5_LANES), im] = ni
            return nr, ni

        zero = jnp.zeros((S5_LANES, half), F32)
        lax.fori_loop(0, lc, step, (zero, zero), unroll=4)

    last = (lc - 1) * S5_LANES
    sub_id = lax.broadcasted_iota(jnp.int32, (S5_LANES, half), 0)
    for c in range(S5_COLS):
        re = slice(c * 2 * half, c * 2 * half + half)
        im = slice(c * 2 * half + half, (c + 1) * 2 * half)
        alr, ali = alr_ref[:, c * half:(c + 1) * half], ali_ref[:, c * half:(c + 1) * half]
        gr, gi = carry[0:1, re], carry[0:1, im]
        g_re = jnp.zeros((S5_LANES, half), F32)
        g_im = jnp.zeros((S5_LANES, half), F32)
        for sub in range(S5_LANES):
            g_re = jnp.where(sub_id == sub, gr, g_re)
            g_im = jnp.where(sub_id == sub, gi, g_im)
            gr, gi = cmul_add(alr, ali, gr, gi, xs[pl.ds(last + sub, 1), re], xs[pl.ds(last + sub, 1), im])
        carry[0:1, re] = gr
        carry[0:1, im] = gi
        ar = jnp.broadcast_to(ar_ref[:, c * half:(c + 1) * half], (S5_LANES, half))
        ai = jnp.broadcast_to(ai_ref[:, c * half:(c + 1) * half], (S5_LANES, half))

        def fix(tau, st, re=re, im=im, ar=ar, ai=ai):
            r0 = pl.multiple_of(tau * S5_LANES, S5_LANES)
            cr, ci = cmul_add(ar, ai, st[0], st[1], 0.0, 0.0)
            xs[pl.ds(r0, S5_LANES), re] = xs[pl.ds(r0, S5_LANES), re] + cr
            xs[pl.ds(r0, S5_LANES), im] = xs[pl.ds(r0, S5_LANES), im] + ci
            return cr, ci

        lax.fori_loop(0, lc, fix, (g_re, g_im), unroll=4)

    ys = []
    for c in range(S5_COLS):
        perm_sc[c] = _dot(xs[:, c * 2 * half:(c + 1) * 2 * half].astype(BF16), wc_ref[c])
        y = jnp.concatenate([perm_sc[c, pl.ds(sub, lc, stride=S5_LANES), :] for sub in range(S5_LANES)], axis=0)
        ys.append(y + d_ref[:, c * LANES:(c + 1) * LANES] * u_refs[c][...])
    y = jnp.concatenate(ys, axis=1)
    y = y * (0.5 * (1.0 + jnp.tanh(math.sqrt(2.0 / math.pi) * (y + 0.044715 * (y * y * y)))))
    o_ref[...] = y * _sigmoid(_dot(y.astype(BF16), gw_ref[...]) + gb_ref[...])


def _s5_params(a_re, a_im, b_re, b_im, c_re, c_im, log_step):
    a = lax.complex(a_re.astype(F32), a_im.astype(F32))
    dt = jnp.exp(log_step.astype(F32))[:, None]
    a_bar = jnp.exp(dt * a)
    b_bar = ((a_bar - 1.0) / a)[..., None] * lax.complex(b_re.astype(F32), b_im.astype(F32))
    a_l = a_bar ** (S5_TILE // S5_LANES)
    gpc = S5_GROUPS // S5_COLS
    eye = jnp.eye(gpc, dtype=F32)

    def in_w(t):
        t = t.reshape(S5_COLS, gpc, S5_STATE, S5_GROUP)
        return jnp.einsum('cgpi,gh->cgihp', t, eye).reshape(S5_COLS, gpc * S5_GROUP, gpc * S5_STATE)

    def out_w(t):
        t = t.reshape(S5_COLS, gpc, S5_GROUP, S5_STATE)
        return jnp.einsum('cgip,gh->cgphi', t, eye).reshape(S5_COLS, gpc * S5_STATE, gpc * S5_GROUP)

    wb = jnp.concatenate([in_w(jnp.real(b_bar)), in_w(jnp.imag(b_bar))], axis=2).astype(BF16)
    wc = jnp.concatenate([out_w(c_re.astype(F32)), -out_w(c_im.astype(F32))], axis=1).astype(BF16)
    flat = lambda t: t.reshape(1, S5_GROUPS * S5_STATE)
    return wb, wc, flat(jnp.real(a_bar)), flat(jnp.imag(a_bar)), flat(jnp.real(a_l)), flat(jnp.imag(a_l))


def _s5(proj, params, d, glu_w, glu_b):
    bsz, seq, _ = proj.shape
    tm = S5_TILE
    wb, wc, ar, ai, alr, ali = params
    nstate = S5_GROUPS * S5_STATE
    base = OFF_S5 // LANES
    in_specs = [pl.BlockSpec((None, tm, LANES), (lambda c: (lambda b, j: (b, j, base + c)))(c)) for c in range(S5_COLS)]
    in_specs += [_resident((S5_COLS, LANES, 2 * S5_WIDTH)), _resident((S5_COLS, 2 * S5_WIDTH, LANES))]
    in_specs += [_resident((1, nstate))] * 4
    in_specs += [_resident((1, S5_WIDTH)), _resident((S5_WIDTH, S5_WIDTH)), _resident((1, S5_WIDTH))]
    return pl.pallas_call(
        _s5_kernel,
        out_shape=jax.ShapeDtypeStruct((bsz, seq, S5_WIDTH), F32),
        grid=(bsz, seq // tm),
        in_specs=in_specs,
        out_specs=pl.BlockSpec((None, tm, S5_WIDTH), lambda b, j: (b, j, 0)),
        scratch_shapes=[pltpu.VMEM((tm, 2 * nstate), F32), pltpu.VMEM((S5_LANES, 2 * nstate), F32),
                        pltpu.VMEM((S5_COLS, tm, LANES), F32)],
        compiler_params=_cparams(("parallel", "arbitrary")),
        name="s5_ssm",
    )(proj, proj, proj, proj, wb, wc, ar, ai, alr, ali, d.reshape(1, S5_WIDTH), glu_w.astype(BF16),
      glu_b.reshape(1, S5_WIDTH))


def _merge_kernel(h_ref, g_ref, ya_ref, yb_ref, yc_ref, o1_ref, o2_ref, o3_ref, l1_ref, l2_ref, l3_ref,
                  wg_ref, wa_ref, wb_ref, wc_ref, wd_ref, wo_ref, out_ref):
    h = h_ref[...]
    ms = jnp.mean(h * h, axis=-1, keepdims=True)
    hn = (h * lax.rsqrt(ms + EPS) * g_ref[...]).astype(BF16)
    l1, l2, l3 = l1_ref[...], l2_ref[...], l3_ref[...]
    mx = jnp.maximum(jnp.maximum(l1, l2), l3)
    e1, e2, e3 = jnp.exp(l1 - mx), jnp.exp(l2 - mx), jnp.exp(l3 - mx)
    tot = e1 + e2 + e3
    yd = (e1 / tot) * o1_ref[...] + (e2 / tot) * o2_ref[...] + (e3 / tot) * o3_ref[...]

    def branch(c, y, w_ref):
        gate = _sigmoid(_dot(hn, wg_ref[:, c * D_MODEL:(c + 1) * D_MODEL]))
        return gate * _dot(y.astype(BF16), w_ref[...])

    merged = (branch(0, ya_ref[...], wa_ref) + branch(1, yb_ref[...], wb_ref) + branch(2, yc_ref[...], wc_ref)
              + branch(3, yd, wd_ref))
    out_ref[...] = h + _dot(merged.astype(BF16), wo_ref[...])


def _merge(h2d, g, ya, yb, yc, dil, wg, wa, wb, wc, wd, wo, tm=256):
    t = h2d.shape[0]
    row = lambda i: (i, 0)
    in_specs = [pl.BlockSpec((tm, D_MODEL), row), _resident((1, D_MODEL))]
    in_specs += [pl.BlockSpec((tm, y.shape[1]), row) for y in (ya, yb, yc)]
    in_specs += [pl.BlockSpec((tm, DIL_OUT), row)] * 6
    in_specs += [_resident(w.shape) for w in (wg, wa, wb, wc, wd, wo)]
    (o1, s1), (o2, s2), (o3, s3) = dil
    return pl.pallas_call(
        _merge_kernel,
        out_shape=jax.ShapeDtypeStruct((t, D_MODEL), F32),
        grid=(t // tm,),
        in_specs=in_specs,
        out_specs=pl.BlockSpec((tm, D_MODEL), row),
        compiler_params=_cparams(("parallel",)),
        name="branch_merge",
    )(h2d, g.reshape(1, D_MODEL), ya, yb, yc, o1, o2, o3, s1, s2, s3, wg, wa, wb, wc, wd, wo)


def _ffn_kernel(h_ref, hp_ref, g_ref, wu_ref, cw_ref, cb_ref, wd_ref, fg_ref, o_ref, hn_sc, z_sc, *, final_norm):
    i = pl.program_id(1)
    tf, halo = FFN_TF, FFN_HALO

    def norm(x):
        ms = jnp.mean(x * x, axis=-1, keepdims=True)
        return x * lax.rsqrt(ms + EPS) * g_ref[...]

    hn_sc[halo:, :] = norm(h_ref[...]).astype(BF16)
    hn_sc[:halo, :] = jnp.where(i == 0, 0.0, norm(hp_ref[...])).astype(BF16)
    hn = hn_sc[...]

    def conv(cols):
        u = _dot(hn, wu_ref[:, cols])
        u1 = pltpu.roll(u, 1, 0)
        u2 = pltpu.roll(u, 2, 0)
        return (cw_ref[0:1, cols] * u2[halo:, :] + cw_ref[1:2, cols] * u1[halo:, :] + cw_ref[2:3, cols] * u[halo:, :]
                + cb_ref[:, cols])

    for c in range(D_FF // tf):
        a = conv(slice(c * tf, (c + 1) * tf))
        b = conv(slice(D_FF + c * tf, D_FF + (c + 1) * tf))
        z_sc[:, c * tf:(c + 1) * tf] = ((a * _sigmoid(a)) * b).astype(BF16)

    y = h_ref[...] + _dot(z_sc[...], wd_ref[...])
    if final_norm:
        ms = jnp.mean(y * y, axis=-1, keepdims=True)
        y = y * lax.rsqrt(ms + EPS) * fg_ref[...]
    o_ref[...] = y


def _ffn(h, g, w_up, conv_w, conv_b, w_down, final_g, final_norm):
    bsz, seq, _ = h.shape
    tm, halo = FFN_TM, FFN_HALO
    return pl.pallas_call(
        functools.partial(_ffn_kernel, final_norm=final_norm),
        out_shape=jax.ShapeDtypeStruct((bsz, seq, D_MODEL), F32),
        grid=(bsz, seq // tm),
        in_specs=[pl.BlockSpec((None, tm, D_MODEL), lambda b, i: (b, i, 0)),
                  pl.BlockSpec((None, halo, D_MODEL), lambda b, i: (b, jnp.maximum(i * (tm // halo) - 1, 0), 0)),
                  _resident((1, D_MODEL)), _resident((D_MODEL, 2 * D_FF)), _resident((CONV_WIDTH, 2 * D_FF)),
                  _resident((1, 2 * D_FF)), _resident((D_FF, D_MODEL)), _resident((1, D_MODEL))],
        out_specs=pl.BlockSpec((None, tm, D_MODEL), lambda b, i: (b, i, 0)),
        scratch_shapes=[pltpu.VMEM((tm + halo, D_MODEL), BF16), pltpu.VMEM((tm, D_FF), BF16)],
        compiler_params=_cparams(("parallel", "arbitrary")),
        name="conv_ffn",
    )(h, h, g.reshape(1, D_MODEL), w_up, conv_w, conv_b.reshape(1, -1), w_down, final_g.reshape(1, D_MODEL))


def kernel(x, norm1_g, w_in, hgrn_lb_logits, hgrn_norm_g, attn_sinks, s5_a_re, s5_a_im, s5_b_re, s5_b_im, s5_c_re, s5_c_im, s5_d, s5_log_step, s5_glu_w, s5_glu_b, w_branch_a, w_branch_b, w_branch_c, w_branch_d, w_out, norm2_g, ffn_w_up, ffn_conv_w, ffn_conv_b, ffn_w_down, final_norm_g):
    bsz, seq, _ = x.shape
    tables = _rope_tables(seq)
    p = jax.nn.softmax(hgrn_lb_logits.astype(F32), axis=0)
    lower_bounds = jnp.cumsum(p, axis=0) - p[0]

    h = x
    for l in range(DEPTH):
        w_in_l = _permute_columns(w_in[l]).astype(BF16)
        proj2d = _in_proj(h.reshape(bsz * seq, D_MODEL), norm1_g[l], w_in_l, tables, seq)
        proj = proj2d.reshape(bsz, seq, D_IN)

        y_a = _hgrn(proj, lower_bounds[l], hgrn_norm_g[l])
        y_b = _band_attn(proj, OFF_SWA_Q, OFF_SWA_K, OFF_SWA_V, SWA_Q_WIDTH, SWA_KV_WIDTH, dil=1, nq=8,
                         max_dist=SWA_WINDOW - 1, sinks=attn_sinks[l].astype(F32))
        y_c = _s5(proj, _s5_params(s5_a_re[l], s5_a_im[l], s5_b_re[l], s5_b_im[l], s5_c_re[l], s5_c_im[l],
                                   s5_log_step[l]), s5_d[l], s5_glu_w[l], s5_glu_b[l])
        dil = []
        for g, (window, dilation) in enumerate(DIL_PAIRS):
            o, lse = _band_attn(proj, OFF_DIL_Q + g * DIL_OUT, OFF_DIL_K + g * DIL_OUT,
                                OFF_DIL_V + g * DIL_OUT, DIL_OUT, DIL_OUT, dil=dilation, nq=DIL_NQ[g],
                                max_dist=window // dilation)
            dil.append((o.reshape(bsz * seq, DIL_OUT), lse.reshape(bsz * seq, DIL_OUT)))

        flat = lambda t: t.reshape(bsz * seq, t.shape[-1])
        h2d = _merge(h.reshape(bsz * seq, D_MODEL), norm1_g[l], flat(y_a), flat(y_b), flat(y_c), dil,
                     w_in[l][:, REF_GATE_OFF:].astype(BF16), w_branch_a[l].astype(BF16), w_branch_b[l].astype(BF16),
                     w_branch_c[l].astype(BF16), w_branch_d[l].astype(BF16), w_out[l].astype(BF16))
        h = _ffn(h2d.reshape(bsz, seq, D_MODEL), norm2_g[l], ffn_w_up[l].astype(BF16), ffn_conv_w[l], ffn_conv_b[l],
                 ffn_w_down[l].astype(BF16), final_norm_g, final_norm=(l == DEPTH - 1))
    return h
```

```python
import functools
import math

import numpy as np
import jax
import jax.numpy as jnp
from jax import lax
from jax.experimental import pallas as pl
from jax.experimental.pallas import tpu as pltpu

F32 = jnp.float32
BF16 = jnp.bfloat16

D_MODEL = 1024
DEPTH = 2
HEAD_DIM = 64
BLOCK = 128
ROPE_THETA = 500000.0
ROT_DIM = HEAD_DIM // 4
EPS = 1e-6
MASK_VALUE = -1e30
LB_FLOOR = 1e-30
N_BRANCH = 4

HGRN_HEADS = 4
HGRN_DIM = 128
HGRN_WIDTH = HGRN_HEADS * HGRN_DIM
HGRN_CHUNK = 128
HGRN_CHUNKS_PER_STEP = 2
HGRN_SUB = 8

SWA_Q_HEADS = 8
SWA_KV_HEADS = 2
SWA_WINDOW = 128
SWA_Q_WIDTH = SWA_Q_HEADS * HEAD_DIM
SWA_KV_WIDTH = SWA_KV_HEADS * HEAD_DIM

S5_WIDTH = 512
S5_GROUP = 16
S5_GROUPS = S5_WIDTH // S5_GROUP
S5_STATE = 64
S5_COLS = 4
S5_TILE = 512
S5_LANES = 8

DIL_PAIRS = ((128, 1), (512, 4), (2048, 16))
DIL_HEADS_PER_GROUP = 4
DIL_HEADS = DIL_HEADS_PER_GROUP * len(DIL_PAIRS)
DIL_OUT = DIL_HEADS_PER_GROUP * HEAD_DIM
DIL_NQ = (8, 2, 1)

D_FF = 2816
CONV_WIDTH = 3
FFN_TF = 256
FFN_TM = 512
FFN_HALO = 16

LANES = 128

OFF_HGRN = 0
OFF_SWA_Q = OFF_HGRN + 4 * HGRN_WIDTH
OFF_S5 = OFF_SWA_Q + SWA_Q_WIDTH
OFF_DIL_Q = OFF_S5 + S5_WIDTH
OFF_DIL_K = OFF_DIL_Q + DIL_HEADS * HEAD_DIM
OFF_DIL_V = OFF_DIL_K + DIL_HEADS * HEAD_DIM
OFF_SWA_K = OFF_DIL_V + DIL_HEADS * HEAD_DIM
OFF_SWA_V = OFF_SWA_K + SWA_KV_WIDTH
D_IN = OFF_SWA_V + SWA_KV_WIDTH

_REF_SIZES = (HGRN_WIDTH,) * 4 + (SWA_Q_WIDTH, SWA_KV_WIDTH, SWA_KV_WIDTH, S5_WIDTH) + (DIL_HEADS * HEAD_DIM,) * 3 + (N_BRANCH * D_MODEL,)
_REF_OFFS = np.concatenate([[0], np.cumsum(_REF_SIZES)])[:-1]
_NEW_OFFS = (OFF_HGRN, OFF_HGRN + 512, OFF_HGRN + 1024, OFF_HGRN + 1536, OFF_SWA_Q, OFF_SWA_K, OFF_SWA_V, OFF_S5,
             OFF_DIL_Q, OFF_DIL_K, OFF_DIL_V)
REF_GATE_OFF = int(_REF_OFFS[-1])

VMEM_LIMIT = 56 * 1024 * 1024


def _permute_columns(w):
    order = np.argsort(np.asarray(_NEW_OFFS))
    return jnp.concatenate([w[..., int(_REF_OFFS[k]):int(_REF_OFFS[k]) + _REF_SIZES[k]] for k in order], axis=-1)


def _cparams(sem):
    return pltpu.CompilerParams(dimension_semantics=sem, vmem_limit_bytes=VMEM_LIMIT)


def _resident(shape):
    return pl.BlockSpec(shape, lambda *_: (0,) * len(shape), pipeline_mode=pl.Buffered(1))


def _sigmoid(x):
    return 1.0 / (1.0 + jnp.exp(-x))


def _split3(x):
    hi = x.astype(BF16)
    r1 = x - hi.astype(F32)
    mid = r1.astype(BF16)
    lo = (r1 - mid.astype(F32)).astype(BF16)
    return hi, mid, lo


def _row_bcast(ref, r, n):
    return jnp.broadcast_to(ref[pl.ds(r, 1), :], (n, ref.shape[1]))


def _dot(a, b):
    return jnp.dot(a, b, preferred_element_type=F32)


def _dot_nt(a, b):
    return lax.dot_general(a, b, (((1,), (1,)), ((), ())), preferred_element_type=F32)


_ROPE_COLS = ((OFF_SWA_Q, SWA_Q_WIDTH, HEAD_DIM ** -0.5), (OFF_DIL_Q, DIL_HEADS * HEAD_DIM, HEAD_DIM ** -0.5),
              (OFF_DIL_K, DIL_HEADS * HEAD_DIM, 1.0), (OFF_SWA_K, SWA_KV_WIDTH, 1.0))


IN_PROJ_SLAB = 256


def _in_proj_kernel(x_ref, g_ref, w_ref, cos_ref, sup_ref, sdn_ref, o_ref):
    x = x_ref[...]
    ms = jnp.mean(x * x, axis=-1, keepdims=True)
    hn = (x * lax.rsqrt(ms + EPS) * g_ref[...]).astype(BF16)
    rope_scale = {c0: sc for a, w, sc in _ROPE_COLS for c0 in range(a, a + w, LANES)}
    for s0 in range(0, D_IN, IN_PROJ_SLAB):
        y = _dot(hn, w_ref[:, s0:s0 + IN_PROJ_SLAB])
        for c0 in range(s0, s0 + IN_PROJ_SLAB, LANES):
            r = y[:, c0 - s0:c0 - s0 + LANES]
            if c0 in rope_scale:
                r = _rope(r, cos_ref[...], sup_ref[...], sdn_ref[...])
                r = r if rope_scale[c0] == 1.0 else r * rope_scale[c0]
            o_ref[:, c0:c0 + LANES] = r


def _in_proj(h2d, g, w_bf16, tables, seq, tm=512):
    t = h2d.shape[0]
    pos = lambda i: (i % (seq // tm), 0)
    return pl.pallas_call(
        _in_proj_kernel,
        out_shape=jax.ShapeDtypeStruct((t, D_IN), F32),
        grid=(t // tm,),
        in_specs=[pl.BlockSpec((tm, D_MODEL), lambda i: (i, 0)), _resident((1, D_MODEL)),
                  _resident((D_MODEL, D_IN))] + [pl.BlockSpec((tm, LANES), pos)] * 3,
        out_specs=pl.BlockSpec((tm, D_IN), lambda i: (i, 0)),
        compiler_params=_cparams(("parallel",)),
        name="in_proj",
    )(h2d, g.reshape(1, D_MODEL), w_bf16, *tables)


def _rope_tables(seq):
    inv_freq = ROPE_THETA ** (-jnp.arange(0, ROT_DIM, 2, dtype=F32) / ROT_DIM)
    ang = jnp.arange(seq, dtype=F32)[:, None] * inv_freq[None, :]
    cos, sin = jnp.cos(ang), jnp.sin(ang)
    half = ROT_DIM // 2
    rest = HEAD_DIM - ROT_DIM
    one, zero = jnp.ones((seq, rest), F32), jnp.zeros((seq, rest), F32)
    zh = jnp.zeros((seq, half), F32)
    cos_h = jnp.concatenate([cos, cos, one], axis=1)
    sup_h = jnp.concatenate([-sin, zh, zero], axis=1)
    sdn_h = jnp.concatenate([zh, sin, zero], axis=1)
    reps = LANES // HEAD_DIM
    return tuple(jnp.tile(t, (1, reps)) for t in (cos_h, sup_h, sdn_h))


def _rope(x, cos, sup, sdn):
    half = ROT_DIM // 2
    return x * cos + pltpu.roll(x, LANES - half, 1) * sup + pltpu.roll(x, half, 1) * sdn


def _band_attn_kernel(*refs, dil, nq, max_dist, has_sink, gqa, pairs):
    if has_sink:
        sink_ref, refs = refs[0], refs[1:]
    q_ref, kc_ref, kp_ref, vc_ref, vp_ref, o_ref = refs[:6]
    refs = refs[6:]
    lse_ref = None
    if not has_sink:
        lse_ref, refs = refs[0], refs[1:]
    j = pl.program_id(1)
    pair0 = pl.program_id(2) * pairs
    blk = BLOCK * dil
    span = nq * blk

    lane = lax.broadcasted_iota(jnp.int32, (1, LANES), 1)
    lo_half = lane < HEAD_DIM

    if gqa:
        k_sc, v_sc = refs
        for sc, prev_ref, cur_ref in ((k_sc, kp_ref, kc_ref), (v_sc, vp_ref, vc_ref)):
            for src, base, n in ((prev_ref, 0, blk), (cur_ref, blk, span)):
                for r0 in range(0, n, min(n, 2 * BLOCK)):
                    rs = slice(r0, r0 + min(n, 2 * BLOCK))
                    t = src[rs, :]
                    sw = pltpu.roll(t, HEAD_DIM, 1)
                    sc[0, base + r0:base + rs.stop, :] = jnp.where(lo_half, t, sw)
                    sc[1, base + r0:base + rs.stop, :] = jnp.where(lo_half, sw, t)

    per_kv = SWA_Q_HEADS // (2 * SWA_KV_HEADS) if gqa else 1
    stack = 2 * per_kv
    qi = lax.broadcasted_iota(jnp.int32, (stack * BLOCK, 2 * BLOCK), 0) & (BLOCK - 1)
    ki = lax.broadcasted_iota(jnp.int32, (stack * BLOCK, 2 * BLOCK), 1)
    dist = qi + BLOCK - ki
    band = (dist >= 0) & (dist <= max_dist)
    head_of_row = lax.broadcasted_iota(jnp.int32, (stack * BLOCK, 1), 0) >> int(math.log2(BLOCK))

    def rows(start, size):
        return pl.ds(start, size) if dil == 1 else pl.ds(start, size, stride=dil)

    def load_kv(cur_ref, prev_ref, sc, i, res, p):
        if gqa:
            return sc[p // (SWA_Q_HEADS // (2 * SWA_KV_HEADS)), i * blk:i * blk + 2 * BLOCK, :]
        cols = slice(p * LANES, (p + 1) * LANES)
        prev = prev_ref[rows(res, BLOCK), cols] if i == 0 else cur_ref[rows((i - 1) * blk + res, BLOCK), cols]
        return jnp.concatenate([prev, cur_ref[rows(i * blk + res, BLOCK), cols]], axis=0)

    for res in range(dil):
        for i in range(nq):
            valid = band & (ki >= jnp.where(j == 0, BLOCK, 0)) if i == 0 else band
            q_rows = rows(i * blk + res, BLOCK)
            for p0 in range(0, pairs, per_kv):
                kp = load_kv(kc_ref, kp_ref, k_sc if gqa else None, i, res, p0).astype(BF16)
                vp = load_kv(vc_ref, vp_ref, v_sc if gqa else None, i, res, p0).astype(BF16)
                qs = []
                for p in range(p0, p0 + per_kv):
                    qp = q_ref[q_rows, p * LANES:(p + 1) * LANES]
                    qs += [jnp.where(lo_half, qp, 0.0), jnp.where(lo_half, 0.0, qp)]
                s = jnp.where(valid, _dot_nt(jnp.concatenate(qs, axis=0).astype(BF16), kp), MASK_VALUE)
                m = jnp.max(s, axis=-1, keepdims=True)
                if has_sink:
                    sk = sink_ref[2 * (pair0 + p0)]
                    for hh in range(1, stack):
                        sk = jnp.where(head_of_row == hh, sink_ref[2 * (pair0 + p0) + hh], sk)
                    m = jnp.maximum(m, sk)
                pe = jnp.exp(s - m)
                den = jnp.sum(pe, axis=-1, keepdims=True)
                if has_sink:
                    den = den + jnp.exp(sk - m)
                o = _dot(pe.astype(BF16), vp) / den
                lse = m + jnp.log(den)
                for p in range(p0, p0 + per_kv):
                    r0 = 2 * (p - p0) * BLOCK
                    cols = slice(p * LANES, (p + 1) * LANES)
                    o_ref[q_rows, cols] = jnp.where(lo_half, o[r0:r0 + BLOCK], o[r0 + BLOCK:r0 + 2 * BLOCK])
                    if lse_ref is not None:
                        lse_ref[q_rows, cols] = jnp.where(lo_half, lse[r0:r0 + BLOCK], lse[r0 + BLOCK:r0 + 2 * BLOCK])


def _band_attn(proj, q_off, k_off, v_off, width, kv_width, dil, nq, max_dist, sinks=None):
    bsz, seq, _ = proj.shape
    blk = BLOCK * dil
    span = nq * blk
    gqa = kv_width != width
    n_pairs = width // LANES
    pairs = n_pairs if dil == 1 else 1
    assert not gqa or (dil == 1 and kv_width == LANES)
    qw = pairs * LANES
    kw = kv_width if gqa else qw
    cur = lambda off, w: (lambda b, j, p: (b, j, off // w + p))
    prev = lambda off, w: (lambda b, j, p: (b, jnp.maximum(j * nq - 1, 0), off // w + p))
    in_specs = [pl.BlockSpec((None, span, qw), cur(q_off, qw)),
                pl.BlockSpec((None, span, kw), cur(k_off, kw)), pl.BlockSpec((None, blk, kw), prev(k_off, kw)),
                pl.BlockSpec((None, span, kw), cur(v_off, kw)), pl.BlockSpec((None, blk, kw), prev(v_off, kw))]
    args = [proj] * 5
    has_sink = sinks is not None
    if has_sink:
        in_specs = [pl.BlockSpec(memory_space=pltpu.SMEM)] + in_specs
        args = [sinks] + args
    o_spec = pl.BlockSpec((None, span, qw), lambda b, j, p: (b, j, p))
    o_shape = jax.ShapeDtypeStruct((bsz, seq, width), F32)
    out_shape, out_specs = (o_shape, o_spec) if has_sink else ((o_shape, o_shape), (o_spec, o_spec))
    return pl.pallas_call(
        functools.partial(_band_attn_kernel, dil=dil, nq=nq, max_dist=max_dist, has_sink=has_sink, gqa=gqa, pairs=pairs),
        out_shape=out_shape,
        grid=(bsz, seq // span, n_pairs // pairs),
        in_specs=in_specs,
        out_specs=out_specs,
        scratch_shapes=[pltpu.VMEM((SWA_KV_HEADS, span + blk, LANES), F32)] * 2 if gqa else [],
        compiler_params=_cparams(("parallel", "arbitrary", "arbitrary")),
        name="swa_attn" if has_sink else f"dil_attn_{dil}",
    )(*args)


def _hgrn_kernel(q_ref, f_ref, i_ref, g_ref, lb_ref, ng_ref, ed_ref, o_ref, st_ref, b_all, k_all):
    L, C, D = HGRN_CHUNK, HGRN_SUB, HGRN_DIM

    @pl.when(pl.program_id(1) == 0)
    def _():
        st_ref[...] = jnp.zeros_like(st_ref)

    row = lax.broadcasted_iota(jnp.int32, (L, L), 0)
    col = lax.broadcasted_iota(jnp.int32, (L, L), 1)
    tri = (col <= row).astype(BF16)
    rowv = lax.broadcasted_iota(jnp.int32, (L, D), 0)
    same_block = lambda size: (row >> int(math.log2(size))) == (col >> int(math.log2(size)))
    diag_mask = same_block(C) & (col <= row)

    for ck, hd in [(ck, hd) for ck in range(HGRN_CHUNKS_PER_STEP) for hd in range(HGRN_HEADS)]:
        cs = slice(hd * D, (hd + 1) * D)
        rs = slice(ck * L, (ck + 1) * L)
        b_sc, k_sc = b_all.at[ck * HGRN_HEADS + hd], k_all.at[ck * HGRN_HEADS + hd]
        x = f_ref[rs, cs]
        lb = lb_ref[:, cs]
        e = jnp.exp(-jnp.abs(x))
        log_sig = jnp.minimum(x, 0.0) - jnp.log(1.0 + e)
        t0 = jnp.log(jnp.maximum(lb, LB_FLOOR))
        t1 = jnp.log1p(-lb) + log_sig
        log_f = jnp.maximum(t0, t1) + jnp.log(1.0 + jnp.exp(-jnp.abs(t0 - t1)))
        kf = (1.0 - lb) * (jnp.where(x > 0.0, e, 1.0) / (1.0 + e))
        qx = q_ref[rs, cs]
        qf = qx * _sigmoid(qx)
        v = i_ref[rs, cs]
        vb = v.astype(BF16)

        hi, mid, lo = _split3(log_f)
        b = (_dot(tri, hi) + _dot(tri, mid) + _dot(tri, lo)) * math.log2(math.e)
        b_sc[...] = b
        k_sc[...] = kf

        scores = jnp.zeros((L, L), F32)
        m = C
        while m < L:
            ref_rows = [_row_bcast(b_sc, s0 + m - 1, 2 * m) for s0 in range(0, L, 2 * m)]
            ref = ref_rows[0] if len(ref_rows) == 1 else jnp.concatenate(ref_rows, axis=0)
            upper = ((rowv >> int(math.log2(m))) & 1) == 1
            w = jnp.exp2(jnp.where(upper, b - ref, ref - b))
            qd = jnp.where(upper, qf * w, 0.0).astype(BF16)
            kd = jnp.where(upper, 0.0, kf * w).astype(BF16)
            scores = scores + jnp.where(same_block(2 * m), _dot_nt(qd, kd), 0.0)
            m *= 2

        xs = []
        for s in range(C):
            bs = jnp.concatenate([_row_bcast(b_sc, blk0 + s, C) for blk0 in range(0, L, C)], axis=0)
            ks = jnp.concatenate([_row_bcast(k_sc, blk0 + s, C) for blk0 in range(0, L, C)], axis=0)
            xs.append((qf * jnp.exp2(jnp.minimum(b - bs, 0.0)) * ks).astype(BF16))
        diag = _dot(jnp.concatenate(xs, axis=1), ed_ref[...])
        scores = scores + jnp.where(diag_mask, diag, 0.0)

        st = st_ref[hd]
        o = _dot(scores.astype(BF16), vb) + _dot_nt((qf * jnp.exp2(b)).astype(BF16), st.astype(BF16))

        b_last = _row_bcast(b_sc, L - 1, L)
        kd_end = (kf * jnp.exp2(b_last - b)).astype(BF16)
        st_ref[hd] = st * jnp.exp2(b_last) + _dot(v.T.astype(BF16), kd_end)

        ms = jnp.mean(o * o, axis=-1, keepdims=True)
        gx = g_ref[rs, cs]
        o_ref[rs, cs] = o * lax.rsqrt(ms + EPS) * ng_ref[...] * (gx * _sigmoid(gx))


def _diag_sum_matrix():
    c, d = HGRN_SUB, HGRN_DIM
    rows_s = np.arange(c * d) // d
    cols = np.arange(HGRN_CHUNK) % c
    return jnp.asarray((rows_s[:, None] == cols[None, :]).astype(np.float32), dtype=BF16)


def _hgrn(proj, lower_bound, norm_g):
    bsz, seq, _ = proj.shape
    L = HGRN_CHUNK
    rows = HGRN_CHUNKS_PER_STEP * L
    base = OFF_HGRN // HGRN_WIDTH
    col = lambda c: (lambda b, j: (b, j, base + c))
    per_chain = pltpu.VMEM((HGRN_CHUNKS_PER_STEP * HGRN_HEADS, L, HGRN_DIM), F32)
    return pl.pallas_call(
        _hgrn_kernel,
        out_shape=jax.ShapeDtypeStruct((bsz, seq, HGRN_WIDTH), F32),
        grid=(bsz, seq // rows),
        in_specs=[pl.BlockSpec((None, rows, HGRN_WIDTH), col(c)) for c in range(4)]
        + [_resident((1, HGRN_WIDTH)), _resident((1, HGRN_DIM)), _resident((HGRN_SUB * HGRN_DIM, L))],
        out_specs=pl.BlockSpec((None, rows, HGRN_WIDTH), lambda b, j: (b, j, 0)),
        scratch_shapes=[pltpu.VMEM((HGRN_HEADS, HGRN_DIM, HGRN_DIM), F32), per_chain, per_chain],
        compiler_params=_cparams(("parallel", "arbitrary")),
        name="hgrn2",
    )(proj, proj, proj, proj, lower_bound.reshape(1, HGRN_WIDTH), norm_g.reshape(1, HGRN_DIM), _diag_sum_matrix())


def _s5_kernel(u0_ref, u1_ref, u2_ref, u3_ref, wb_ref, wc_ref, ar_ref, ai_ref, alr_ref, ali_ref,
               d_ref, gw_ref, gb_ref, o_ref, xs, carry, perm_sc):
    tm = S5_TILE
    lc = tm // S5_LANES
    half = S5_WIDTH
    u_refs = (u0_ref, u1_ref, u2_ref, u3_ref)

    @pl.when(pl.program_id(1) == 0)
    def _():
        carry[...] = jnp.zeros_like(carry)

    for c in range(S5_COLS):
        for tau in range(lc):
            perm_sc[c, tau * S5_LANES:(tau + 1) * S5_LANES, :] = u_refs[c][pl.ds(tau, S5_LANES, stride=lc), :]
        xs[:, c * 2 * half:(c + 1) * 2 * half] = _dot(perm_sc[c].astype(BF16), wb_ref[c])

    def cmul_add(ar, ai, xr, xi, br, bi):
        return ar * xr - ai * xi + br, ar * xi + ai * xr + bi

    for c in range(S5_COLS):
        re = slice(c * 2 * half, c * 2 * half + half)
        im = slice(c * 2 * half + half, (c + 1) * 2 * half)
        ar = jnp.broadcast_to(ar_ref[:, c * half:(c + 1) * half], (S5_LANES, half))
        ai = jnp.broadcast_to(ai_ref[:, c * half:(c + 1) * half], (S5_LANES, half))

        sr = si = jnp.zeros((S5_LANES, half), F32)
        for tau in range(lc):
            rows = slice(tau * S5_LANES, (tau + 1) * S5_LANES)
            sr, si = cmul_add(ar, ai, sr, si, xs[rows, re], xs[rows, im])
            xs[rows, re] = sr
            xs[rows, im] = si

    last = (lc - 1) * S5_LANES
    sub_id = lax.broadcasted_iota(jnp.int32, (S5_LANES, half), 0)
    for c in range(S5_COLS):
        re = slice(c * 2 * half, c * 2 * half + half)
        im = slice(c * 2 * half + half, (c + 1) * 2 * half)
        alr, ali = alr_ref[:, c * half:(c + 1) * half], ali_ref[:, c * half:(c + 1) * half]
        gr, gi = carry[0:1, re], carry[0:1, im]
        g_re = jnp.zeros((S5_LANES, half), F32)
        g_im = jnp.zeros((S5_LANES, half), F32)
        for sub in range(S5_LANES):
            g_re = jnp.where(sub_id == sub, gr, g_re)
            g_im = jnp.where(sub_id == sub, gi, g_im)
            gr, gi = cmul_add(alr, ali, gr, gi, xs[pl.ds(last + sub, 1), re], xs[pl.ds(last + sub, 1), im])
        carry[0:1, re] = gr
        carry[0:1, im] = gi
        ar = jnp.broadcast_to(ar_ref[:, c * half:(c + 1) * half], (S5_LANES, half))
        ai = jnp.broadcast_to(ai_ref[:, c * half:(c + 1) * half], (S5_LANES, half))

        cr, ci = g_re, g_im
        for tau in range(lc):
            rows = slice(tau * S5_LANES, (tau + 1) * S5_LANES)
            cr, ci = cmul_add(ar, ai, cr, ci, 0.0, 0.0)
            xs[rows, re] = xs[rows, re] + cr
            xs[rows, im] = xs[rows, im] + ci

    ys = []
    for c in range(S5_COLS):
        perm_sc[c] = _dot(xs[:, c * 2 * half:(c + 1) * 2 * half].astype(BF16), wc_ref[c])
        y = jnp.concatenate([perm_sc[c, pl.ds(sub, lc, stride=S5_LANES), :] for sub in range(S5_LANES)], axis=0)
        ys.append(y + d_ref[:, c * LANES:(c + 1) * LANES] * u_refs[c][...])
    y = jnp.concatenate(ys, axis=1)
    y = y * (0.5 * (1.0 + jnp.tanh(math.sqrt(2.0 / math.pi) * (y + 0.044715 * (y * y * y)))))
    o_ref[...] = y * _sigmoid(_dot(y.astype(BF16), gw_ref[...]) + gb_ref[...])


def _s5_params(a_re, a_im, b_re, b_im, c_re, c_im, log_step):
    a = lax.complex(a_re.astype(F32), a_im.astype(F32))
    dt = jnp.exp(log_step.astype(F32))[:, None]
    a_bar = jnp.exp(dt * a)
    b_bar = ((a_bar - 1.0) / a)[..., None] * lax.complex(b_re.astype(F32), b_im.astype(F32))
    a_l = a_bar ** (S5_TILE // S5_LANES)
    gpc = S5_GROUPS // S5_COLS
    eye = jnp.eye(gpc, dtype=F32)

    def in_w(t):
        t = t.reshape(S5_COLS, gpc, S5_STATE, S5_GROUP)
        return jnp.einsum('cgpi,gh->cgihp', t, eye).reshape(S5_COLS, gpc * S5_GROUP, gpc * S5_STATE)

    def out_w(t):
        t = t.reshape(S5_COLS, gpc, S5_GROUP, S5_STATE)
        return jnp.einsum('cgip,gh->cgphi', t, eye).reshape(S5_COLS, gpc * S5_STATE, gpc * S5_GROUP)

    wb = jnp.concatenate([in_w(jnp.real(b_bar)), in_w(jnp.imag(b_bar))], axis=2).astype(BF16)
    wc = jnp.concatenate([out_w(c_re.astype(F32)), -out_w(c_im.astype(F32))], axis=1).astype(BF16)
    flat = lambda t: t.reshape(1, S5_GROUPS * S5_STATE)
    return wb, wc, flat(jnp.real(a_bar)), flat(jnp.imag(a_bar)), flat(jnp.real(a_l)), flat(jnp.imag(a_l))


def _s5(proj, params, d, glu_w, glu_b):
    bsz, seq, _ = proj.shape
    tm = S5_TILE
    wb, wc, ar, ai, alr, ali = params
    nstate = S5_GROUPS * S5_STATE
    base = OFF_S5 // LANES
    in_specs = [pl.BlockSpec((None, tm, LANES), (lambda c: (lambda b, j: (b, j, base + c)))(c)) for c in range(S5_COLS)]
    in_specs += [_resident((S5_COLS, LANES, 2 * S5_WIDTH)), _resident((S5_COLS, 2 * S5_WIDTH, LANES))]
    in_specs += [_resident((1, nstate))] * 4
    in_specs += [_resident((1, S5_WIDTH)), _resident((S5_WIDTH, S5_WIDTH)), _resident((1, S5_WIDTH))]
    return pl.pallas_call(
        _s5_kernel,
        out_shape=jax.ShapeDtypeStruct((bsz, seq, S5_WIDTH), F32),
        grid=(bsz, seq // tm),
        in_specs=in_specs,
        out_specs=pl.BlockSpec((None, tm, S5_WIDTH), lambda b, j: (b, j, 0)),
        scratch_shapes=[pltpu.VMEM((tm, 2 * nstate), F32), pltpu.VMEM((S5_LANES, 2 * nstate), F32),
                        pltpu.VMEM((S5_COLS, tm, LANES), F32)],
        compiler_params=_cparams(("parallel", "arbitrary")),
        name="s5_ssm",
    )(proj, proj, proj, proj, wb, wc, ar, ai, alr, ali, d.reshape(1, S5_WIDTH), glu_w.astype(BF16),
      glu_b.reshape(1, S5_WIDTH))


def _merge_kernel(h_ref, g_ref, ya_ref, yb_ref, yc_ref, o1_ref, o2_ref, o3_ref, l1_ref, l2_ref, l3_ref,
                  wg_ref, wa_ref, wb_ref, wc_ref, wd_ref, wo_ref, out_ref):
    h = h_ref[...]
    ms = jnp.mean(h * h, axis=-1, keepdims=True)
    hn = (h * lax.rsqrt(ms + EPS) * g_ref[...]).astype(BF16)
    l1, l2, l3 = l1_ref[...], l2_ref[...], l3_ref[...]
    mx = jnp.maximum(jnp.maximum(l1, l2), l3)
    e1, e2, e3 = jnp.exp(l1 - mx), jnp.exp(l2 - mx), jnp.exp(l3 - mx)
    tot = e1 + e2 + e3
    yd = (e1 / tot) * o1_ref[...] + (e2 / tot) * o2_ref[...] + (e3 / tot) * o3_ref[...]

    def branch(c, y, w_ref):
        gate = _sigmoid(_dot(hn, wg_ref[:, c * D_MODEL:(c + 1) * D_MODEL]))
        return gate * _dot(y.astype(BF16), w_ref[...])

    merged = (branch(0, ya_ref[...], wa_ref) + branch(1, yb_ref[...], wb_ref) + branch(2, yc_ref[...], wc_ref)
              + branch(3, yd, wd_ref))
    out_ref[...] = h + _dot(merged.astype(BF16), wo_ref[...])


def _merge(h2d, g, ya, yb, yc, dil, wg, wa, wb, wc, wd, wo, tm=512):
    t = h2d.shape[0]
    row = lambda i: (i, 0)
    in_specs = [pl.BlockSpec((tm, D_MODEL), row), _resident((1, D_MODEL))]
    in_specs += [pl.BlockSpec((tm, y.shape[1]), row) for y in (ya, yb, yc)]
    in_specs += [pl.BlockSpec((tm, DIL_OUT), row)] * 6
    in_specs += [_resident(w.shape) for w in (wg, wa, wb, wc, wd, wo)]
    (o1, s1), (o2, s2), (o3, s3) = dil
    return pl.pallas_call(
        _merge_kernel,
        out_shape=jax.ShapeDtypeStruct((t, D_MODEL), F32),
        grid=(t // tm,),
        in_specs=in_specs,
        out_specs=pl.BlockSpec((tm, D_MODEL), row),
        compiler_params=_cparams(("parallel",)),
        name="branch_merge",
    )(h2d, g.reshape(1, D_MODEL), ya, yb, yc, o1, o2, o3, s1, s2, s3, wg, wa, wb, wc, wd, wo)


def _ffn_kernel(h_ref, hp_ref, g_ref, wu_ref, cw_ref, cb_ref, wd_ref, fg_ref, o_ref, hn_sc, z_sc, *, final_norm):
    i = pl.program_id(1)
    tf, halo = FFN_TF, FFN_HALO

    def norm(x):
        ms = jnp.mean(x * x, axis=-1, keepdims=True)
        return x * lax.rsqrt(ms + EPS) * g_ref[...]

    hn_sc[halo:, :] = norm(h_ref[...]).astype(BF16)
    hn_sc[:halo, :] = jnp.where(i == 0, 0.0, norm(hp_ref[...])).astype(BF16)
    hn = hn_sc[...]

    def conv(cols):
        u = _dot(hn, wu_ref[:, cols])
        u1 = pltpu.roll(u, 1, 0)
        u2 = pltpu.roll(u, 2, 0)
        return (cw_ref[0:1, cols] * u2[halo:, :] + cw_ref[1:2, cols] * u1[halo:, :] + cw_ref[2:3, cols] * u[halo:, :]
                + cb_ref[:, cols])

    for c in range(D_FF // tf):
        a = conv(slice(c * tf, (c + 1) * tf))
        b = conv(slice(D_FF + c * tf, D_FF + (c + 1) * tf))
        z_sc[:, c * tf:(c + 1) * tf] = ((a * _sigmoid(a)) * b).astype(BF16)

    y = h_ref[...] + _dot(z_sc[...], wd_ref[...])
    if final_norm:
        ms = jnp.mean(y * y, axis=-1, keepdims=True)
        y = y * lax.rsqrt(ms + EPS) * fg_ref[...]
    o_ref[...] = y


def _ffn(h, g, w_up, conv_w, conv_b, w_down, final_g, final_norm):
    bsz, seq, _ = h.shape
    tm, halo = FFN_TM, FFN_HALO
    return pl.pallas_call(
        functools.partial(_ffn_kernel, final_norm=final_norm),
        out_shape=jax.ShapeDtypeStruct((bsz, seq, D_MODEL), F32),
        grid=(bsz, seq // tm),
        in_specs=[pl.BlockSpec((None, tm, D_MODEL), lambda b, i: (b, i, 0)),
                  pl.BlockSpec((None, halo, D_MODEL), lambda b, i: (b, jnp.maximum(i * (tm // halo) - 1, 0), 0)),
                  _resident((1, D_MODEL)), _resident((D_MODEL, 2 * D_FF)), _resident((CONV_WIDTH, 2 * D_FF)),
                  _resident((1, 2 * D_FF)), _resident((D_FF, D_MODEL)), _resident((1, D_MODEL))],
        out_specs=pl.BlockSpec((None, tm, D_MODEL), lambda b, i: (b, i, 0)),
        scratch_shapes=[pltpu.VMEM((tm + halo, D_MODEL), BF16), pltpu.VMEM((tm, D_FF), BF16)],
        compiler_params=_cparams(("parallel", "arbitrary")),
        name="conv_ffn",
    )(h, h, g.reshape(1, D_MODEL), w_up, conv_w, conv_b.reshape(1, -1), w_down, final_g.reshape(1, D_MODEL))


def kernel(x, norm1_g, w_in, hgrn_lb_logits, hgrn_norm_g, attn_sinks, s5_a_re, s5_a_im, s5_b_re, s5_b_im, s5_c_re, s5_c_im, s5_d, s5_log_step, s5_glu_w, s5_glu_b, w_branch_a, w_branch_b, w_branch_c, w_branch_d, w_out, norm2_g, ffn_w_up, ffn_conv_w, ffn_conv_b, ffn_w_down, final_norm_g):
    bsz, seq, _ = x.shape
    tables = _rope_tables(seq)
    p = jax.nn.softmax(hgrn_lb_logits.astype(F32), axis=0)
    lower_bounds = jnp.cumsum(p, axis=0) - p[0]

    h = x
    for l in range(DEPTH):
        w_in_l = _permute_columns(w_in[l]).astype(BF16)
        proj2d = _in_proj(h.reshape(bsz * seq, D_MODEL), norm1_g[l], w_in_l, tables, seq)
        proj = proj2d.reshape(bsz, seq, D_IN)

        y_a = _hgrn(proj, lower_bounds[l], hgrn_norm_g[l])
        y_b = _band_attn(proj, OFF_SWA_Q, OFF_SWA_K, OFF_SWA_V, SWA_Q_WIDTH, SWA_KV_WIDTH, dil=1, nq=8,
                         max_dist=SWA_WINDOW - 1, sinks=attn_sinks[l].astype(F32))
        y_c = _s5(proj, _s5_params(s5_a_re[l], s5_a_im[l], s5_b_re[l], s5_b_im[l], s5_c_re[l], s5_c_im[l],
                                   s5_log_step[l]), s5_d[l], s5_glu_w[l], s5_glu_b[l])
        dil = []
        for g, (window, dilation) in enumerate(DIL_PAIRS):
            o, lse = _band_attn(proj, OFF_DIL_Q + g * DIL_OUT, OFF_DIL_K + g * DIL_OUT,
                                OFF_DIL_V + g * DIL_OUT, DIL_OUT, DIL_OUT, dil=dilation, nq=DIL_NQ[g],
                                max_dist=window // dilation)
            dil.append((o.reshape(bsz * seq, DIL_OUT), lse.reshape(bsz * seq, DIL_OUT)))

        flat = lambda t: t.reshape(bsz * seq, t.shape[-1])
        h2d = _merge(h.reshape(bsz * seq, D_MODEL), norm1_g[l], flat(y_a), flat(y_b), flat(y_c), dil,
                     w_in[l][:, REF_GATE_OFF:].astype(BF16), w_branch_a[l].astype(BF16), w_branch_b[l].astype(BF16),
                     w_branch_c[l].astype(BF16), w_branch_d[l].astype(BF16), w_out[l].astype(BF16))
        h = _ffn(h2d.reshape(bsz, seq, D_MODEL), norm2_g[l], ffn_w_up[l].astype(BF16), ffn_conv_w[l], ffn_conv_b[l],
                 ffn_w_down[l].astype(BF16), final_norm_g, final_norm=(l == DEPTH - 1))
    return h
```

```python
import functools
import math

import numpy as np
import jax
import jax.numpy as jnp
from jax import lax
from jax.experimental import pallas as pl
from jax.experimental.pallas import tpu as pltpu

F32 = jnp.float32
BF16 = jnp.bfloat16

D_MODEL = 1024
DEPTH = 2
HEAD_DIM = 64
BLOCK = 128
ROPE_THETA = 500000.0
ROT_DIM = HEAD_DIM // 4
EPS = 1e-6
MASK_VALUE = -1e30
LB_FLOOR = 1e-30
N_BRANCH = 4

HGRN_HEADS = 4
HGRN_DIM = 128
HGRN_WIDTH = HGRN_HEADS * HGRN_DIM
HGRN_CHUNK = 128
HGRN_SUB = 8

SWA_Q_HEADS = 8
SWA_KV_HEADS = 2
SWA_WINDOW = 128
SWA_Q_WIDTH = SWA_Q_HEADS * HEAD_DIM
SWA_KV_WIDTH = SWA_KV_HEADS * HEAD_DIM

S5_WIDTH = 512
S5_GROUP = 16
S5_GROUPS = S5_WIDTH // S5_GROUP
S5_STATE = 64
S5_COLS = 4
S5_TILE = 512
S5_LANES = 8

DIL_PAIRS = ((128, 1), (512, 4), (2048, 16))
DIL_HEADS_PER_GROUP = 4
DIL_HEADS = DIL_HEADS_PER_GROUP * len(DIL_PAIRS)
DIL_OUT = DIL_HEADS_PER_GROUP * HEAD_DIM
DIL_NQ = (8, 2, 1)

D_FF = 2816
CONV_WIDTH = 3
FFN_TF = 256
FFN_TM = 512
FFN_HALO = 16

LANES = 128

HGRN_COLS = 4 * HGRN_WIDTH
OFF_SWA_Q = 0
OFF_S5 = OFF_SWA_Q + SWA_Q_WIDTH
OFF_DIL_Q = OFF_S5 + S5_WIDTH
OFF_DIL_K = OFF_DIL_Q + DIL_HEADS * HEAD_DIM
OFF_DIL_V = OFF_DIL_K + DIL_HEADS * HEAD_DIM
OFF_SWA_K = OFF_DIL_V + DIL_HEADS * HEAD_DIM
OFF_SWA_V = OFF_SWA_K + SWA_KV_WIDTH
D_PROJ = OFF_SWA_V + SWA_KV_WIDTH
D_IN = HGRN_COLS + D_PROJ

_REF_SIZES = (HGRN_WIDTH,) * 4 + (SWA_Q_WIDTH, SWA_KV_WIDTH, SWA_KV_WIDTH, S5_WIDTH) + (DIL_HEADS * HEAD_DIM,) * 3 + (N_BRANCH * D_MODEL,)
_REF_OFFS = np.concatenate([[0], np.cumsum(_REF_SIZES)])[:-1]
_NEW_OFFS = (0, 512, 1024, 1536) + tuple(HGRN_COLS + o for o in (OFF_SWA_Q, OFF_SWA_K, OFF_SWA_V, OFF_S5, OFF_DIL_Q,
                                                                    OFF_DIL_K, OFF_DIL_V))
REF_GATE_OFF = int(_REF_OFFS[-1])

VMEM_LIMIT = 56 * 1024 * 1024


def _permute_columns(w):
    order = np.argsort(np.asarray(_NEW_OFFS))
    return jnp.concatenate([w[..., int(_REF_OFFS[k]):int(_REF_OFFS[k]) + _REF_SIZES[k]] for k in order], axis=-1)


def _cparams(sem):
    return pltpu.CompilerParams(dimension_semantics=sem, vmem_limit_bytes=VMEM_LIMIT)


def _resident(shape):
    return pl.BlockSpec(shape, lambda *_: (0,) * len(shape), pipeline_mode=pl.Buffered(1))


def _sigmoid(x):
    return 0.5 * jnp.tanh(0.5 * x) + 0.5


def _split3(x):
    hi = x.astype(BF16)
    r1 = x - hi.astype(F32)
    mid = r1.astype(BF16)
    lo = (r1 - mid.astype(F32)).astype(BF16)
    return hi, mid, lo


def _row_bcast(ref, r, n):
    return jnp.broadcast_to(ref[pl.ds(r, 1), :], (n, ref.shape[1]))


def _dot(a, b):
    return jnp.dot(a, b, preferred_element_type=F32)


def _dot_nt(a, b):
    return lax.dot_general(a, b, (((1,), (1,)), ((), ())), preferred_element_type=F32)


_ROPE_COLS = ((OFF_SWA_Q, SWA_Q_WIDTH, HEAD_DIM ** -0.5), (OFF_DIL_Q, DIL_HEADS * HEAD_DIM, HEAD_DIM ** -0.5),
              (OFF_DIL_K, DIL_HEADS * HEAD_DIM, 1.0), (OFF_SWA_K, SWA_KV_WIDTH, 1.0))


IN_PROJ_SLAB = 256
IN_PROJ_TM = 512


def _rope_tables(seq):
    half = ROT_DIM // 2
    inv_freq = ROPE_THETA ** (-jnp.arange(0, ROT_DIM, 2, dtype=F32) / ROT_DIM)
    dim = np.arange(LANES) % HEAD_DIM
    ang = jnp.arange(seq, dtype=F32)[:, None] * inv_freq[dim % half][None, :]
    cos, sin = jnp.cos(ang), jnp.sin(ang)
    cos_t = jnp.where(dim < ROT_DIM, cos, 1.0)
    sup_t = jnp.where(dim < half, -sin, 0.0)
    sdn_t = jnp.where((dim >= half) & (dim < ROT_DIM), sin, 0.0)
    return cos_t, sup_t, sdn_t


def _rope(x, cos, sup, sdn):
    half = ROT_DIM // 2
    return x * cos + pltpu.roll(x, LANES - half, 1) * sup + pltpu.roll(x, half, 1) * sdn


def _band_attn_kernel(*refs, dil, nq, max_dist, has_sink, gqa, pairs):
    if has_sink:
        sink_ref, refs = refs[0], refs[1:]
    q_ref, kc_ref, kp_ref, vc_ref, vp_ref, o_ref = refs[:6]
    refs = refs[6:]
    lse_ref = None
    if not has_sink:
        lse_ref, refs = refs[0], refs[1:]
    j = pl.program_id(1)
    pair0 = pl.program_id(2) * pairs
    blk = BLOCK * dil
    span = nq * blk

    lane = lax.broadcasted_iota(jnp.int32, (1, LANES), 1)
    lo_half = lane < HEAD_DIM

    if gqa:
        k_sc, v_sc = refs
        for sc, prev_ref, cur_ref in ((k_sc, kp_ref, kc_ref), (v_sc, vp_ref, vc_ref)):
            for src, base, n in ((prev_ref, 0, blk), (cur_ref, blk, span)):
                for r0 in range(0, n, min(n, 2 * BLOCK)):
                    rs = slice(r0, r0 + min(n, 2 * BLOCK))
                    t = src[rs, :]
                    sw = pltpu.roll(t, HEAD_DIM, 1)
                    sc[0, base + r0:base + rs.stop, :] = jnp.where(lo_half, t, sw)
                    sc[1, base + r0:base + rs.stop, :] = jnp.where(lo_half, sw, t)

    per_kv = SWA_Q_HEADS // (2 * SWA_KV_HEADS) if gqa else 1
    stack = 2 * per_kv
    qi = lax.broadcasted_iota(jnp.int32, (stack * BLOCK, 2 * BLOCK), 0) & (BLOCK - 1)
    ki = lax.broadcasted_iota(jnp.int32, (stack * BLOCK, 2 * BLOCK), 1)
    dist = qi + BLOCK - ki
    band = (dist >= 0) & (dist <= max_dist)
    head_of_row = lax.broadcasted_iota(jnp.int32, (stack * BLOCK, 1), 0) >> int(math.log2(BLOCK))

    def rows(start, size):
        return pl.ds(start, size) if dil == 1 else pl.ds(start, size, stride=dil)

    def load_kv(cur_ref, prev_ref, sc, i, res, p):
        if gqa:
            return sc[p // (SWA_Q_HEADS // (2 * SWA_KV_HEADS)), i * blk:i * blk + 2 * BLOCK, :]
        cols = slice(p * LANES, (p + 1) * LANES)
        prev = prev_ref[rows(res, BLOCK), cols] if i == 0 else cur_ref[rows((i - 1) * blk + res, BLOCK), cols]
        return jnp.concatenate([prev, cur_ref[rows(i * blk + res, BLOCK), cols]], axis=0)

    for res in range(dil):
        for i in range(nq):
            valid = band & (ki >= jnp.where(j == 0, BLOCK, 0)) if i == 0 else band
            q_rows = rows(i * blk + res, BLOCK)
            for p0 in range(0, pairs, per_kv):
                kp = load_kv(kc_ref, kp_ref, k_sc if gqa else None, i, res, p0).astype(BF16)
                vp = load_kv(vc_ref, vp_ref, v_sc if gqa else None, i, res, p0).astype(BF16)
                qs = []
                for p in range(p0, p0 + per_kv):
                    qp = q_ref[q_rows, p * LANES:(p + 1) * LANES]
                    qs += [jnp.where(lo_half, qp, 0.0), jnp.where(lo_half, 0.0, qp)]
                s = jnp.where(valid, _dot_nt(jnp.concatenate(qs, axis=0).astype(BF16), kp), MASK_VALUE)
                m = jnp.max(s, axis=-1, keepdims=True)
                if has_sink:
                    sk = sink_ref[2 * (pair0 + p0)]
                    for hh in range(1, stack):
                        sk = jnp.where(head_of_row == hh, sink_ref[2 * (pair0 + p0) + hh], sk)
                    m = jnp.maximum(m, sk)
                pe = jnp.exp(s - m)
                den = jnp.sum(pe, axis=-1, keepdims=True)
                if has_sink:
                    den = den + jnp.exp(sk - m)
                o = _dot(pe.astype(BF16), vp) / den
                lse = m + jnp.log(den)
                for p in range(p0, p0 + per_kv):
                    r0 = 2 * (p - p0) * BLOCK
                    cols = slice(p * LANES, (p + 1) * LANES)
                    o_ref[q_rows, cols] = jnp.where(lo_half, o[r0:r0 + BLOCK], o[r0 + BLOCK:r0 + 2 * BLOCK])
                    if lse_ref is not None:
                        lse_ref[q_rows, cols] = jnp.where(lo_half, lse[r0:r0 + BLOCK], lse[r0 + BLOCK:r0 + 2 * BLOCK])


def _band_attn(proj, q_off, k_off, v_off, width, kv_width, dil, nq, max_dist, sinks=None):
    bsz, seq, _ = proj.shape
    blk = BLOCK * dil
    span = nq * blk
    gqa = kv_width != width
    n_pairs = width // LANES
    pairs = n_pairs if dil == 1 else 1
    assert not gqa or (dil == 1 and kv_width == LANES)
    qw = pairs * LANES
    kw = kv_width if gqa else qw
    cur = lambda off, w: (lambda b, j, p: (b, j, off // w + p))
    prev = lambda off, w: (lambda b, j, p: (b, jnp.maximum(j * nq - 1, 0), off // w + p))
    in_specs = [pl.BlockSpec((None, span, qw), cur(q_off, qw)),
                pl.BlockSpec((None, span, kw), cur(k_off, kw)), pl.BlockSpec((None, blk, kw), prev(k_off, kw)),
                pl.BlockSpec((None, span, kw), cur(v_off, kw)), pl.BlockSpec((None, blk, kw), prev(v_off, kw))]
    args = [proj] * 5
    has_sink = sinks is not None
    if has_sink:
        in_specs = [pl.BlockSpec(memory_space=pltpu.SMEM)] + in_specs
        args = [sinks] + args
    o_spec = pl.BlockSpec((None, span, qw), lambda b, j, p: (b, j, p))
    o_shape = jax.ShapeDtypeStruct((bsz, seq, width), F32)
    out_shape, out_specs = (o_shape, o_spec) if has_sink else ((o_shape, o_shape), (o_spec, o_spec))
    return pl.pallas_call(
        functools.partial(_band_attn_kernel, dil=dil, nq=nq, max_dist=max_dist, has_sink=has_sink, gqa=gqa, pairs=pairs),
        out_shape=out_shape,
        grid=(bsz, seq // span, n_pairs // pairs),
        in_specs=in_specs,
        out_specs=out_specs,
        scratch_shapes=[pltpu.VMEM((SWA_KV_HEADS, span + blk, LANES), F32)] * 2 if gqa else [],
        compiler_params=_cparams(("parallel", "arbitrary", "arbitrary")),
        name="swa_attn" if has_sink else f"dil_attn_{dil}",
    )(*args)


def _hgrn_chunks(src_ref, lb_ref, ng_ref, ed_ref, o_ref, st_ref, b_all, k_all, c_all, n_chunks, between):
    L, C, D = HGRN_CHUNK, HGRN_SUB, HGRN_DIM

    row = lax.broadcasted_iota(jnp.int32, (L, L), 0)
    col = lax.broadcasted_iota(jnp.int32, (L, L), 1)
    tri = (col <= row).astype(BF16)
    rowv = lax.broadcasted_iota(jnp.int32, (L, D), 0)
    same_block = lambda size: (row >> int(math.log2(size))) == (col >> int(math.log2(size)))
    diag_mask = same_block(C) & (col <= row)

    for ck, hd in [(ck, hd) for ck in range(n_chunks) for hd in range(HGRN_HEADS)]:
        cs = slice(hd * D, (hd + 1) * D)
        rs = slice(ck * L, (ck + 1) * L)
        src = lambda kind, rs=rs, hd=hd: src_ref[rs, kind * HGRN_WIDTH + hd * D:kind * HGRN_WIDTH + (hd + 1) * D]
        n = ck * HGRN_HEADS + hd
        b_sc, k_sc, c_sc = b_all.at[n], k_all.at[n], c_all.at[n]
        x = src(1)
        lb = lb_ref[:, cs]
        e = jnp.exp(-jnp.abs(x))
        log_sig = jnp.minimum(x, 0.0) - jnp.log(1.0 + e)
        t0 = jnp.log(jnp.maximum(lb, LB_FLOOR))
        t1 = jnp.log1p(-lb) + log_sig
        log_f = jnp.maximum(t0, t1) + jnp.log(1.0 + jnp.exp(-jnp.abs(t0 - t1)))
        kf = (1.0 - lb) * (jnp.where(x > 0.0, e, 1.0) / (1.0 + e))
        qx = src(0)
        qf = qx * _sigmoid(qx)
        v = src(2)
        vb = v.astype(BF16)

        hi, mid, lo = _split3(log_f)
        b = (_dot(tri, hi) + _dot(tri, mid) + _dot(tri, lo)) * math.log2(math.e)
        b_sc[...] = b
        lk = jnp.log2(kf)
        k_sc[...] = lk
        c_sc[...] = lk - b

        scores = jnp.zeros((L, L), F32)
        m = C
        while m < L:
            ref_rows = [_row_bcast(b_sc, s0 + m - 1, 2 * m) for s0 in range(0, L, 2 * m)]
            ref = ref_rows[0] if len(ref_rows) == 1 else jnp.concatenate(ref_rows, axis=0)
            upper = ((rowv >> int(math.log2(m))) & 1) == 1
            w = jnp.exp2(jnp.where(upper, b - ref, ref - b))
            qd = jnp.where(upper, qf * w, 0.0).astype(BF16)
            kd = jnp.where(upper, 0.0, kf * w).astype(BF16)
            scores = scores + jnp.where(same_block(2 * m), _dot_nt(qd, kd), 0.0)
            m *= 2

        xs = []
        for s in range(C):
            cs_ = jnp.concatenate([_row_bcast(c_sc, blk0 + s, C) for blk0 in range(0, L, C)], axis=0)
            ks = jnp.concatenate([_row_bcast(k_sc, blk0 + s, C) for blk0 in range(0, L, C)], axis=0)
            xs.append((qf * jnp.exp2(jnp.minimum(b + cs_, ks))).astype(BF16))
        diag = _dot(jnp.concatenate(xs, axis=1), ed_ref[...])
        scores = scores + jnp.where(diag_mask, diag, 0.0)

        st = st_ref[hd]
        o = _dot(scores.astype(BF16), vb) + _dot_nt((qf * jnp.exp2(b)).astype(BF16), st.astype(BF16))

        b_last = _row_bcast(b_sc, L - 1, L)
        kd_end = (kf * jnp.exp2(b_last - b)).astype(BF16)
        st_ref[hd] = st * jnp.exp2(b_last) + _dot(v.T.astype(BF16), kd_end)

        ms = jnp.mean(o * o, axis=-1, keepdims=True)
        gx = src(3)
        o_ref[rs, cs] = o * lax.rsqrt(ms + EPS) * ng_ref[...] * (gx * _sigmoid(gx))
        between(n, n_chunks * HGRN_HEADS)


def _diag_sum_matrix():
    c, d = HGRN_SUB, HGRN_DIM
    rows_s = np.arange(c * d) // d
    cols = np.arange(HGRN_CHUNK) % c
    return jnp.asarray((rows_s[:, None] == cols[None, :]).astype(np.float32), dtype=BF16)


def _proj_hgrn_kernel(x_ref, g_ref, w_ref, cos_ref, sup_ref, sdn_ref, lb_ref, ng_ref, ed_ref, o_ref, ya_ref,
                      st_ref, h_sc, b_all, k_all, c_all):
    @pl.when(pl.program_id(1) == 0)
    def _():
        st_ref[...] = jnp.zeros_like(st_ref)

    x = x_ref[...]
    ms = jnp.mean(x * x, axis=-1, keepdims=True)
    hn = (x * lax.rsqrt(ms + EPS) * g_ref[...]).astype(BF16)
    for s0 in range(0, HGRN_COLS, IN_PROJ_SLAB):
        h_sc[:, s0:s0 + IN_PROJ_SLAB] = _dot(hn, w_ref[:, s0:s0 + IN_PROJ_SLAB])

    rope_scale = {c0: sc for a, w, sc in _ROPE_COLS for c0 in range(a, a + w, LANES)}
    slabs = list(range(0, D_PROJ, IN_PROJ_SLAB))

    def project_slab(s0):
        y = _dot(hn, w_ref[:, HGRN_COLS + s0:HGRN_COLS + s0 + IN_PROJ_SLAB])
        for c0 in range(s0, s0 + IN_PROJ_SLAB, LANES):
            r = y[:, c0 - s0:c0 - s0 + LANES]
            if c0 in rope_scale:
                r = _rope(r, cos_ref[...], sup_ref[...], sdn_ref[...])
                r = r if rope_scale[c0] == 1.0 else r * rope_scale[c0]
            o_ref[:, c0:c0 + LANES] = r

    def between(n, total):
        for k in range(n * len(slabs) // total, (n + 1) * len(slabs) // total):
            project_slab(slabs[k])

    _hgrn_chunks(h_sc, lb_ref, ng_ref, ed_ref, ya_ref, st_ref, b_all, k_all, c_all, IN_PROJ_TM // HGRN_CHUNK, between)


def _proj_hgrn(h, g, w_bf16, tables, lower_bound, norm_g):
    bsz, seq, _ = h.shape
    tm, L = IN_PROJ_TM, HGRN_CHUNK
    chains = (tm // L) * HGRN_HEADS
    per_chain = pltpu.VMEM((chains, L, HGRN_DIM), F32)
    tile = lambda w: pl.BlockSpec((None, tm, w), lambda b, j: (b, j, 0))
    return pl.pallas_call(
        _proj_hgrn_kernel,
        out_shape=(jax.ShapeDtypeStruct((bsz, seq, D_PROJ), F32), jax.ShapeDtypeStruct((bsz, seq, HGRN_WIDTH), F32)),
        grid=(bsz, seq // tm),
        in_specs=[tile(D_MODEL), _resident((1, D_MODEL)), _resident((D_MODEL, D_IN))]
        + [pl.BlockSpec((tm, LANES), lambda b, j: (j, 0))] * 3
        + [_resident((1, HGRN_WIDTH)), _resident((1, HGRN_DIM)), _resident((HGRN_SUB * HGRN_DIM, L))],
        out_specs=(tile(D_PROJ), tile(HGRN_WIDTH)),
        scratch_shapes=[pltpu.VMEM((HGRN_HEADS, HGRN_DIM, HGRN_DIM), F32), pltpu.VMEM((tm, HGRN_COLS), F32),
                        per_chain, per_chain, per_chain],
        compiler_params=_cparams(("parallel", "arbitrary")),
        name="proj_hgrn",
    )(h, g.reshape(1, D_MODEL), w_bf16, *tables, lower_bound.reshape(1, HGRN_WIDTH), norm_g.reshape(1, HGRN_DIM),
      _diag_sum_matrix())


def _s5_kernel(u0_ref, u1_ref, u2_ref, u3_ref, wb_ref, wc_ref, ar_ref, ai_ref, alr_ref, ali_ref,
               d_ref, gw_ref, gb_ref, o_ref, xs, carry, perm_sc):
    tm = S5_TILE
    lc = tm // S5_LANES
    half = S5_WIDTH
    u_refs = (u0_ref, u1_ref, u2_ref, u3_ref)

    @pl.when(pl.program_id(1) == 0)
    def _():
        carry[...] = jnp.zeros_like(carry)

    for c in range(S5_COLS):
        for tau in range(lc):
            perm_sc[c, tau * S5_LANES:(tau + 1) * S5_LANES, :] = u_refs[c][pl.ds(tau, S5_LANES, stride=lc), :]
        xs[:, c * 2 * half:(c + 1) * 2 * half] = _dot(perm_sc[c].astype(BF16), wb_ref[c])

    def cmul_add(ar, ai, xr, xi, br, bi):
        return ar * xr - ai * xi + br, ar * xi + ai * xr + bi

    for c in range(S5_COLS):
        re = slice(c * 2 * half, c * 2 * half + half)
        im = slice(c * 2 * half + half, (c + 1) * 2 * half)
        ar = jnp.broadcast_to(ar_ref[:, c * half:(c + 1) * half], (S5_LANES, half))
        ai = jnp.broadcast_to(ai_ref[:, c * half:(c + 1) * half], (S5_LANES, half))

        sr = si = jnp.zeros((S5_LANES, half), F32)
        for tau in range(lc):
            rows = slice(tau * S5_LANES, (tau + 1) * S5_LANES)
            sr, si = cmul_add(ar, ai, sr, si, xs[rows, re], xs[rows, im])
            xs[rows, re] = sr
            xs[rows, im] = si

    last = (lc - 1) * S5_LANES
    sub_id = lax.broadcasted_iota(jnp.int32, (S5_LANES, half), 0)
    for c in range(S5_COLS):
        re = slice(c * 2 * half, c * 2 * half + half)
        im = slice(c * 2 * half + half, (c + 1) * 2 * half)
        alr, ali = alr_ref[:, c * half:(c + 1) * half], ali_ref[:, c * half:(c + 1) * half]
        gr, gi = carry[0:1, re], carry[0:1, im]
        g_re = jnp.zeros((S5_LANES, half), F32)
        g_im = jnp.zeros((S5_LANES, half), F32)
        for sub in range(S5_LANES):
            g_re = jnp.where(sub_id == sub, gr, g_re)
            g_im = jnp.where(sub_id == sub, gi, g_im)
            gr, gi = cmul_add(alr, ali, gr, gi, xs[pl.ds(last + sub, 1), re], xs[pl.ds(last + sub, 1), im])
        carry[0:1, re] = gr
        carry[0:1, im] = gi
        ar = jnp.broadcast_to(ar_ref[:, c * half:(c + 1) * half], (S5_LANES, half))
        ai = jnp.broadcast_to(ai_ref[:, c * half:(c + 1) * half], (S5_LANES, half))

        cr, ci = g_re, g_im
        for tau in range(lc):
            rows = slice(tau * S5_LANES, (tau + 1) * S5_LANES)
            cr, ci = cmul_add(ar, ai, cr, ci, 0.0, 0.0)
            xs[rows, re] = xs[rows, re] + cr
            xs[rows, im] = xs[rows, im] + ci

    ys = []
    for c in range(S5_COLS):
        perm_sc[c] = _dot(xs[:, c * 2 * half:(c + 1) * 2 * half].astype(BF16), wc_ref[c])
        y = jnp.concatenate([perm_sc[c, pl.ds(sub, lc, stride=S5_LANES), :] for sub in range(S5_LANES)], axis=0)
        ys.append(y + d_ref[:, c * LANES:(c + 1) * LANES] * u_refs[c][...])
    y = jnp.concatenate(ys, axis=1)
    y = y * (0.5 * (1.0 + jnp.tanh(math.sqrt(2.0 / math.pi) * (y + 0.044715 * (y * y * y)))))
    o_ref[...] = y * _sigmoid(_dot(y.astype(BF16), gw_ref[...]) + gb_ref[...])


def _s5_params(a_re, a_im, b_re, b_im, c_re, c_im, log_step):
    a = lax.complex(a_re.astype(F32), a_im.astype(F32))
    dt = jnp.exp(log_step.astype(F32))[:, None]
    a_bar = jnp.exp(dt * a)
    b_bar = ((a_bar - 1.0) / a)[..., None] * lax.complex(b_re.astype(F32), b_im.astype(F32))
    a_l = a_bar ** (S5_TILE // S5_LANES)
    gpc = S5_GROUPS // S5_COLS
    eye = jnp.eye(gpc, dtype=F32)

    def in_w(t):
        t = t.reshape(S5_COLS, gpc, S5_STATE, S5_GROUP)
        return jnp.einsum('cgpi,gh->cgihp', t, eye).reshape(S5_COLS, gpc * S5_GROUP, gpc * S5_STATE)

    def out_w(t):
        t = t.reshape(S5_COLS, gpc, S5_GROUP, S5_STATE)
        return jnp.einsum('cgip,gh->cgphi', t, eye).reshape(S5_COLS, gpc * S5_STATE, gpc * S5_GROUP)

    wb = jnp.concatenate([in_w(jnp.real(b_bar)), in_w(jnp.imag(b_bar))], axis=2).astype(BF16)
    wc = jnp.concatenate([out_w(c_re.astype(F32)), -out_w(c_im.astype(F32))], axis=1).astype(BF16)
    flat = lambda t: t.reshape(1, S5_GROUPS * S5_STATE)
    return wb, wc, flat(jnp.real(a_bar)), flat(jnp.imag(a_bar)), flat(jnp.real(a_l)), flat(jnp.imag(a_l))


def _s5(proj, params, d, glu_w, glu_b):
    bsz, seq, _ = proj.shape
    tm = S5_TILE
    wb, wc, ar, ai, alr, ali = params
    nstate = S5_GROUPS * S5_STATE
    base = OFF_S5 // LANES
    in_specs = [pl.BlockSpec((None, tm, LANES), (lambda c: (lambda b, j: (b, j, base + c)))(c)) for c in range(S5_COLS)]
    in_specs += [_resident((S5_COLS, LANES, 2 * S5_WIDTH)), _resident((S5_COLS, 2 * S5_WIDTH, LANES))]
    in_specs += [_resident((1, nstate))] * 4
    in_specs += [_resident((1, S5_WIDTH)), _resident((S5_WIDTH, S5_WIDTH)), _resident((1, S5_WIDTH))]
    return pl.pallas_call(
        _s5_kernel,
        out_shape=jax.ShapeDtypeStruct((bsz, seq, S5_WIDTH), F32),
        grid=(bsz, seq // tm),
        in_specs=in_specs,
        out_specs=pl.BlockSpec((None, tm, S5_WIDTH), lambda b, j: (b, j, 0)),
        scratch_shapes=[pltpu.VMEM((tm, 2 * nstate), F32), pltpu.VMEM((S5_LANES, 2 * nstate), F32),
                        pltpu.VMEM((S5_COLS, tm, LANES), F32)],
        compiler_params=_cparams(("parallel", "arbitrary")),
        name="s5_ssm",
    )(proj, proj, proj, proj, wb, wc, ar, ai, alr, ali, d.reshape(1, S5_WIDTH), glu_w.astype(BF16),
      glu_b.reshape(1, S5_WIDTH))


def _merge_kernel(h_ref, g_ref, ya_ref, yb_ref, yc_ref, o1_ref, o2_ref, o3_ref, l1_ref, l2_ref, l3_ref,
                  wg_ref, wa_ref, wb_ref, wc_ref, wd_ref, wo_ref, out_ref):
    h = h_ref[...]
    ms = jnp.mean(h * h, axis=-1, keepdims=True)
    hn = (h * lax.rsqrt(ms + EPS) * g_ref[...]).astype(BF16)
    l1, l2, l3 = l1_ref[...], l2_ref[...], l3_ref[...]
    mx = jnp.maximum(jnp.maximum(l1, l2), l3)
    e1, e2, e3 = jnp.exp(l1 - mx), jnp.exp(l2 - mx), jnp.exp(l3 - mx)
    tot = e1 + e2 + e3
    yd = (e1 / tot) * o1_ref[...] + (e2 / tot) * o2_ref[...] + (e3 / tot) * o3_ref[...]

    def branch(c, y, w_ref):
        gate = _sigmoid(_dot(hn, wg_ref[:, c * D_MODEL:(c + 1) * D_MODEL]))
        return gate * _dot(y.astype(BF16), w_ref[...])

    merged = (branch(0, ya_ref[...], wa_ref) + branch(1, yb_ref[...], wb_ref) + branch(2, yc_ref[...], wc_ref)
              + branch(3, yd, wd_ref))
    out_ref[...] = h + _dot(merged.astype(BF16), wo_ref[...])


def _merge(h2d, g, ya, yb, yc, dil, wg, wa, wb, wc, wd, wo, tm=512):
    t = h2d.shape[0]
    row = lambda i: (i, 0)
    in_specs = [pl.BlockSpec((tm, D_MODEL), row), _resident((1, D_MODEL))]
    in_specs += [pl.BlockSpec((tm, y.shape[1]), row) for y in (ya, yb, yc)]
    in_specs += [pl.BlockSpec((tm, DIL_OUT), row)] * 6
    in_specs += [_resident(w.shape) for w in (wg, wa, wb, wc, wd, wo)]
    (o1, s1), (o2, s2), (o3, s3) = dil
    return pl.pallas_call(
        _merge_kernel,
        out_shape=jax.ShapeDtypeStruct((t, D_MODEL), F32),
        grid=(t // tm,),
        in_specs=in_specs,
        out_specs=pl.BlockSpec((tm, D_MODEL), row),
        compiler_params=_cparams(("parallel",)),
        name="branch_merge",
    )(h2d, g.reshape(1, D_MODEL), ya, yb, yc, o1, o2, o3, s1, s2, s3, wg, wa, wb, wc, wd, wo)


def _ffn_kernel(h_ref, hp_ref, g_ref, wu_ref, cw_ref, cb_ref, wd_ref, fg_ref, o_ref, hn_sc, z_sc, *, final_norm):
    i = pl.program_id(1)
    tf, halo = FFN_TF, FFN_HALO

    def norm(x):
        ms = jnp.mean(x * x, axis=-1, keepdims=True)
        return x * lax.rsqrt(ms + EPS) * g_ref[...]

    hn_sc[halo:, :] = norm(h_ref[...]).astype(BF16)
    hn_sc[:halo, :] = jnp.where(i == 0, 0.0, norm(hp_ref[...])).astype(BF16)

    def conv(cols):
        u = _dot(hn_sc[...], wu_ref[:, cols])
        u1 = pltpu.roll(u, 1, 0)
        u2 = pltpu.roll(u, 2, 0)
        return (cw_ref[0:1, cols] * u2[halo:, :] + cw_ref[1:2, cols] * u1[halo:, :] + cw_ref[2:3, cols] * u[halo:, :]
                + cb_ref[:, cols])

    for c in range(D_FF // tf):
        a = conv(slice(c * tf, (c + 1) * tf))
        b = conv(slice(D_FF + c * tf, D_FF + (c + 1) * tf))
        z_sc[:, c * tf:(c + 1) * tf] = ((a * _sigmoid(a)) * b).astype(BF16)

    y = h_ref[...] + _dot(z_sc[...], wd_ref[...])
    if final_norm:
        ms = jnp.mean(y * y, axis=-1, keepdims=True)
        y = y * lax.rsqrt(ms + EPS) * fg_ref[...]
    o_ref[...] = y


def _ffn(h, g, w_up, conv_w, conv_b, w_down, final_g, final_norm):
    bsz, seq, _ = h.shape
    tm, halo = FFN_TM, FFN_HALO
    return pl.pallas_call(
        functools.partial(_ffn_kernel, final_norm=final_norm),
        out_shape=jax.ShapeDtypeStruct((bsz, seq, D_MODEL), F32),
        grid=(bsz, seq // tm),
        in_specs=[pl.BlockSpec((None, tm, D_MODEL), lambda b, i: (b, i, 0)),
                  pl.BlockSpec((None, halo, D_MODEL), lambda b, i: (b, jnp.maximum(i * (tm // halo) - 1, 0), 0)),
                  _resident((1, D_MODEL)), _resident((D_MODEL, 2 * D_FF)), _resident((CONV_WIDTH, 2 * D_FF)),
                  _resident((1, 2 * D_FF)), _resident((D_FF, D_MODEL)), _resident((1, D_MODEL))],
        out_specs=pl.BlockSpec((None, tm, D_MODEL), lambda b, i: (b, i, 0)),
        scratch_shapes=[pltpu.VMEM((tm + halo, D_MODEL), BF16), pltpu.VMEM((tm, D_FF), BF16)],
        compiler_params=_cparams(("parallel", "arbitrary")),
        name="conv_ffn",
    )(h, h, g.reshape(1, D_MODEL), w_up, conv_w, conv_b.reshape(1, -1), w_down, final_g.reshape(1, D_MODEL))


def kernel(x, norm1_g, w_in, hgrn_lb_logits, hgrn_norm_g, attn_sinks, s5_a_re, s5_a_im, s5_b_re, s5_b_im, s5_c_re, s5_c_im, s5_d, s5_log_step, s5_glu_w, s5_glu_b, w_branch_a, w_branch_b, w_branch_c, w_branch_d, w_out, norm2_g, ffn_w_up, ffn_conv_w, ffn_conv_b, ffn_w_down, final_norm_g):
    bsz, seq, _ = x.shape
    tables = _rope_tables(seq)
    p = jax.nn.softmax(hgrn_lb_logits.astype(F32), axis=0)
    lower_bounds = jnp.cumsum(p, axis=0) - p[0]

    h = x
    for l in range(DEPTH):
        w_in_l = _permute_columns(w_in[l]).astype(BF16)
        proj, y_a = _proj_hgrn(h, norm1_g[l], w_in_l, tables, lower_bounds[l], hgrn_norm_g[l])
        y_b = _band_attn(proj, OFF_SWA_Q, OFF_SWA_K, OFF_SWA_V, SWA_Q_WIDTH, SWA_KV_WIDTH, dil=1, nq=8,
                         max_dist=SWA_WINDOW - 1, sinks=attn_sinks[l].astype(F32))
        y_c = _s5(proj, _s5_params(s5_a_re[l], s5_a_im[l], s5_b_re[l], s5_b_im[l], s5_c_re[l], s5_c_im[l],
                                   s5_log_step[l]), s5_d[l], s5_glu_w[l], s5_glu_b[l])
        dil = []
        for g, (window, dilation) in enumerate(DIL_PAIRS):
            o, lse = _band_attn(proj, OFF_DIL_Q + g * DIL_OUT, OFF_DIL_K + g * DIL_OUT,
                                OFF_DIL_V + g * DIL_OUT, DIL_OUT, DIL_OUT, dil=dilation, nq=DIL_NQ[g],
                                max_dist=window // dilation)
            dil.append((o.reshape(bsz * seq, DIL_OUT), lse.reshape(bsz * seq, DIL_OUT)))

        flat = lambda t: t.reshape(bsz * seq, t.shape[-1])
        h2d = _merge(h.reshape(bsz * seq, D_MODEL), norm1_g[l], flat(y_a), flat(y_b), flat(y_c), dil,
                     w_in[l][:, REF_GATE_OFF:].astype(BF16), w_branch_a[l].astype(BF16), w_branch_b[l].astype(BF16),
                     w_branch_c[l].astype(BF16), w_branch_d[l].astype(BF16), w_out[l].astype(BF16))
        h = _ffn(h2d.reshape(bsz, seq, D_MODEL), norm2_g[l], ffn_w_up[l].astype(BF16), ffn_conv_w[l], ffn_conv_b[l],
                 ffn_w_down[l].astype(BF16), final_norm_g, final_norm=(l == DEPTH - 1))
    return h
```

```python
import functools
import math

import numpy as np
import jax
import jax.numpy as jnp
from jax import lax
from jax.experimental import pallas as pl
from jax.experimental.pallas import tpu as pltpu

F32 = jnp.float32
BF16 = jnp.bfloat16

D_MODEL = 1024
DEPTH = 2
HEAD_DIM = 64
BLOCK = 128
ROPE_THETA = 500000.0
ROT_DIM = HEAD_DIM // 4
EPS = 1e-6
MASK_VALUE = -1e30
LB_FLOOR = 1e-30
N_BRANCH = 4

HGRN_HEADS = 4
HGRN_DIM = 128
HGRN_WIDTH = HGRN_HEADS * HGRN_DIM
HGRN_CHUNK = 128
HGRN_SUB = 8

SWA_Q_HEADS = 8
SWA_KV_HEADS = 2
SWA_WINDOW = 128
SWA_Q_WIDTH = SWA_Q_HEADS * HEAD_DIM
SWA_KV_WIDTH = SWA_KV_HEADS * HEAD_DIM

S5_WIDTH = 512
S5_GROUP = 16
S5_GROUPS = S5_WIDTH // S5_GROUP
S5_STATE = 64
S5_COLS = 4
S5_TILE = 512
S5_LANES = 8

DIL_PAIRS = ((128, 1), (512, 4), (2048, 16))
DIL_HEADS_PER_GROUP = 4
DIL_HEADS = DIL_HEADS_PER_GROUP * len(DIL_PAIRS)
DIL_OUT = DIL_HEADS_PER_GROUP * HEAD_DIM
DIL_NQ = (8, 2, 1)

D_FF = 2816
CONV_WIDTH = 3
FFN_TF = 256
FFN_TM = 512
FFN_SUBTILES = 2
FFN_HALO = 16

LANES = 128

HGRN_COLS = 4 * HGRN_WIDTH
OFF_SWA_Q = 0
OFF_S5 = OFF_SWA_Q + SWA_Q_WIDTH
OFF_DIL_Q = OFF_S5 + S5_WIDTH
OFF_DIL_K = OFF_DIL_Q + DIL_HEADS * HEAD_DIM
OFF_DIL_V = OFF_DIL_K + DIL_HEADS * HEAD_DIM
OFF_SWA_K = OFF_DIL_V + DIL_HEADS * HEAD_DIM
OFF_SWA_V = OFF_SWA_K + SWA_KV_WIDTH
D_PROJ = OFF_SWA_V + SWA_KV_WIDTH
D_IN = HGRN_COLS + D_PROJ

_REF_SIZES = (HGRN_WIDTH,) * 4 + (SWA_Q_WIDTH, SWA_KV_WIDTH, SWA_KV_WIDTH, S5_WIDTH) + (DIL_HEADS * HEAD_DIM,) * 3 + (N_BRANCH * D_MODEL,)
_REF_OFFS = np.concatenate([[0], np.cumsum(_REF_SIZES)])[:-1]
_NEW_OFFS = (0, 512, 1024, 1536) + tuple(HGRN_COLS + o for o in (OFF_SWA_Q, OFF_SWA_K, OFF_SWA_V, OFF_S5, OFF_DIL_Q,
                                                                    OFF_DIL_K, OFF_DIL_V))
REF_GATE_OFF = int(_REF_OFFS[-1])

VMEM_LIMIT = 56 * 1024 * 1024
VMEM_LIMIT_S5_MERGE = 60 * 1024 * 1024


def _permute_columns(w):
    order = np.argsort(np.asarray(_NEW_OFFS))
    parts = [w[..., int(_REF_OFFS[k]):int(_REF_OFFS[k]) + _REF_SIZES[k]] for k in order]
    hgrn = [p[..., hd * HGRN_DIM:(hd + 1) * HGRN_DIM] for hd in range(HGRN_HEADS) for p in parts[:4]]
    return jnp.concatenate(hgrn + parts[4:], axis=-1)


def _cparams(sem, vmem_limit=VMEM_LIMIT):
    return pltpu.CompilerParams(dimension_semantics=sem, vmem_limit_bytes=vmem_limit)


def _resident(shape):
    return pl.BlockSpec(shape, lambda *_: (0,) * len(shape), pipeline_mode=pl.Buffered(1))


def _sigmoid(x):
    return 0.5 * jnp.tanh(0.5 * x) + 0.5


def _row_bcast(ref, r, n):
    return jnp.broadcast_to(ref[pl.ds(r, 1), :], (n, ref.shape[1]))


def _dot(a, b):
    return jnp.dot(a, b, preferred_element_type=F32)


def _dot_nt(a, b):
    return lax.dot_general(a, b, (((1,), (1,)), ((), ())), preferred_element_type=F32)


_ROPE_COLS = ((OFF_SWA_Q, SWA_Q_WIDTH, HEAD_DIM ** -0.5), (OFF_DIL_Q, DIL_HEADS * HEAD_DIM, HEAD_DIM ** -0.5),
              (OFF_DIL_K, DIL_HEADS * HEAD_DIM, 1.0), (OFF_SWA_K, SWA_KV_WIDTH, 1.0))


IN_PROJ_SLAB = 256
IN_PROJ_TM = 512


def _rope_tables(seq):
    half = ROT_DIM // 2
    inv_freq = ROPE_THETA ** (-jnp.arange(0, ROT_DIM, 2, dtype=F32) / ROT_DIM)
    dim = np.arange(LANES) % HEAD_DIM
    ang = jnp.arange(seq, dtype=F32)[:, None] * inv_freq[dim % half][None, :]
    cos, sin = jnp.cos(ang), jnp.sin(ang)
    cos_t = jnp.where(dim < ROT_DIM, cos, 1.0)
    sup_t = jnp.where(dim < half, -sin, 0.0)
    sdn_t = jnp.where((dim >= half) & (dim < ROT_DIM), sin, 0.0)
    return cos_t, sup_t, sdn_t


def _rope(x, cos, sup, sdn):
    half = ROT_DIM // 2
    return x * cos + pltpu.roll(x, LANES - half, 1) * sup + pltpu.roll(x, half, 1) * sdn


def _band_attn_kernel(*refs, dil, nq, max_dist, has_sink, gqa, pairs):
    if has_sink:
        sink_ref, refs = refs[0], refs[1:]
    q_ref, kc_ref, kp_ref, vc_ref, vp_ref, o_ref = refs[:6]
    refs = refs[6:]
    lse_ref = None
    if not has_sink:
        lse_ref, refs = refs[0], refs[1:]
    j = pl.program_id(1)
    pair0 = pl.program_id(2) * pairs
    blk = BLOCK * dil
    span = nq * blk

    lane = lax.broadcasted_iota(jnp.int32, (1, LANES), 1)
    lo_half = lane < HEAD_DIM

    if gqa:
        k_sc, v_sc = refs
        for sc, prev_ref, cur_ref in ((k_sc, kp_ref, kc_ref), (v_sc, vp_ref, vc_ref)):
            for src, base, n in ((prev_ref, 0, blk), (cur_ref, blk, span)):
                for r0 in range(0, n, min(n, 2 * BLOCK)):
                    rs = slice(r0, r0 + min(n, 2 * BLOCK))
                    t = src[rs, :]
                    sw = pltpu.roll(t, HEAD_DIM, 1)
                    sc[0, base + r0:base + rs.stop, :] = jnp.where(lo_half, t, sw)
                    sc[1, base + r0:base + rs.stop, :] = jnp.where(lo_half, sw, t)

    per_kv = SWA_Q_HEADS // (2 * SWA_KV_HEADS) if gqa else 1
    stack = 2 * per_kv
    qi = lax.broadcasted_iota(jnp.int32, (stack * BLOCK, 2 * BLOCK), 0) & (BLOCK - 1)
    ki = lax.broadcasted_iota(jnp.int32, (stack * BLOCK, 2 * BLOCK), 1)
    dist = qi + BLOCK - ki
    band = (dist >= 0) & (dist <= max_dist)
    head_of_row = lax.broadcasted_iota(jnp.int32, (stack * BLOCK, 1), 0) >> int(math.log2(BLOCK))

    def rows(start, size):
        return pl.ds(start, size) if dil == 1 else pl.ds(start, size, stride=dil)

    def load_kv(cur_ref, prev_ref, sc, i, res, p):
        if gqa:
            return sc[p // (SWA_Q_HEADS // (2 * SWA_KV_HEADS)), i * blk:i * blk + 2 * BLOCK, :]
        cols = slice(p * LANES, (p + 1) * LANES)
        prev = prev_ref[rows(res, BLOCK), cols] if i == 0 else cur_ref[rows((i - 1) * blk + res, BLOCK), cols]
        return jnp.concatenate([prev, cur_ref[rows(i * blk + res, BLOCK), cols]], axis=0)

    for res in range(dil):
        for i in range(nq):
            valid = band & (ki >= jnp.where(j == 0, BLOCK, 0)) if i == 0 else band
            q_rows = rows(i * blk + res, BLOCK)
            for p0 in range(0, pairs, per_kv):
                kp = load_kv(kc_ref, kp_ref, k_sc if gqa else None, i, res, p0).astype(BF16)
                vp = load_kv(vc_ref, vp_ref, v_sc if gqa else None, i, res, p0).astype(BF16)
                qs = []
                for p in range(p0, p0 + per_kv):
                    qp = q_ref[q_rows, p * LANES:(p + 1) * LANES]
                    qs += [jnp.where(lo_half, qp, 0.0), jnp.where(lo_half, 0.0, qp)]
                s = jnp.where(valid, _dot_nt(jnp.concatenate(qs, axis=0).astype(BF16), kp), MASK_VALUE)
                m = jnp.max(s, axis=-1, keepdims=True)
                if has_sink:
                    sk = sink_ref[2 * (pair0 + p0)]
                    for hh in range(1, stack):
                        sk = jnp.where(head_of_row == hh, sink_ref[2 * (pair0 + p0) + hh], sk)
                    m = jnp.maximum(m, sk)
                pe = jnp.exp(s - m)
                den = jnp.sum(pe, axis=-1, keepdims=True)
                if has_sink:
                    den = den + jnp.exp(sk - m)
                o = _dot(pe.astype(BF16), vp) / den
                lse = m + jnp.log(den)
                for p in range(p0, p0 + per_kv):
                    r0 = 2 * (p - p0) * BLOCK
                    cols = slice(p * LANES, (p + 1) * LANES)
                    o_ref[q_rows, cols] = jnp.where(lo_half, o[r0:r0 + BLOCK], o[r0 + BLOCK:r0 + 2 * BLOCK])
                    if lse_ref is not None:
                        lse_ref[q_rows, cols] = jnp.where(lo_half, lse[r0:r0 + BLOCK], lse[r0 + BLOCK:r0 + 2 * BLOCK])


def _band_attn(proj, q_off, k_off, v_off, width, kv_width, dil, nq, max_dist, sinks=None):
    bsz, seq, _ = proj.shape
    blk = BLOCK * dil
    span = nq * blk
    gqa = kv_width != width
    n_pairs = width // LANES
    pairs = n_pairs if dil == 1 else 1
    assert not gqa or (dil == 1 and kv_width == LANES)
    qw = pairs * LANES
    kw = kv_width if gqa else qw
    cur = lambda off, w: (lambda b, j, p: (b, j, off // w + p))
    prev = lambda off, w: (lambda b, j, p: (b, jnp.maximum(j * nq - 1, 0), off // w + p))
    in_specs = [pl.BlockSpec((None, span, qw), cur(q_off, qw)),
                pl.BlockSpec((None, span, kw), cur(k_off, kw)), pl.BlockSpec((None, blk, kw), prev(k_off, kw)),
                pl.BlockSpec((None, span, kw), cur(v_off, kw)), pl.BlockSpec((None, blk, kw), prev(v_off, kw))]
    args = [proj] * 5
    has_sink = sinks is not None
    if has_sink:
        in_specs = [pl.BlockSpec(memory_space=pltpu.SMEM)] + in_specs
        args = [sinks] + args
    o_spec = pl.BlockSpec((None, span, qw), lambda b, j, p: (b, j, p))
    o_shape = jax.ShapeDtypeStruct((bsz, seq, width), F32)
    out_shape, out_specs = (o_shape, o_spec) if has_sink else ((o_shape, o_shape), (o_spec, o_spec))
    return pl.pallas_call(
        functools.partial(_band_attn_kernel, dil=dil, nq=nq, max_dist=max_dist, has_sink=has_sink, gqa=gqa, pairs=pairs),
        out_shape=out_shape,
        grid=(bsz, seq // span, n_pairs // pairs),
        in_specs=in_specs,
        out_specs=out_specs,
        scratch_shapes=[pltpu.VMEM((SWA_KV_HEADS, span + blk, LANES), F32)] * 2 if gqa else [],
        compiler_params=_cparams(("parallel", "arbitrary", "arbitrary")),
        name="swa_attn" if has_sink else f"dil_attn_{dil}",
    )(*args)


def _hgrn_chunks(src_ref, lb_ref, ng_ref, ed_ref, o_ref, st_ref, b_all, k_all, c_all, n_chunks, between):
    L, C, D = HGRN_CHUNK, HGRN_SUB, HGRN_DIM

    row = lax.broadcasted_iota(jnp.int32, (L, L), 0)
    col = lax.broadcasted_iota(jnp.int32, (L, L), 1)
    rowv = lax.broadcasted_iota(jnp.int32, (L, D), 0)
    same_block = lambda size: (row >> int(math.log2(size))) == (col >> int(math.log2(size)))
    diag_mask = same_block(C) & (col <= row)

    for ck, hd in [(ck, hd) for ck in range(n_chunks) for hd in range(HGRN_HEADS)]:
        cs = slice(hd * D, (hd + 1) * D)
        rs = slice(ck * L, (ck + 1) * L)
        src = lambda kind, rs=rs, hd=hd: src_ref[rs, (4 * hd + kind) * D:(4 * hd + kind + 1) * D]
        n = ck * HGRN_HEADS + hd
        b_sc, k_sc, c_sc = b_all.at[n], k_all.at[n], c_all.at[n]
        x = src(1)
        lb = lb_ref[:, cs]
        e = jnp.exp(-jnp.abs(x))
        log_sig = jnp.minimum(x, 0.0) - jnp.log(1.0 + e)
        t0 = jnp.log(jnp.maximum(lb, LB_FLOOR))
        t1 = jnp.log1p(-lb) + log_sig
        log_f = jnp.maximum(t0, t1) + jnp.log(1.0 + jnp.exp(-jnp.abs(t0 - t1)))
        kf = (1.0 - lb) * (jnp.where(x > 0.0, e, 1.0) / (1.0 + e))
        qx = src(0)
        qf = qx * _sigmoid(qx)
        v = src(2)
        vb = v.astype(BF16)

        b = log_f * math.log2(math.e)
        shift = 1
        while shift < L:
            b = b + jnp.where(rowv >= shift, pltpu.roll(b, shift, 0), 0.0)
            shift *= 2
        b_sc[...] = b
        lk = jnp.log2(kf)
        k_sc[...] = lk
        c_sc[...] = lk - b

        scores = jnp.zeros((L, L), F32)
        m = C
        while m < L:
            ref_rows = [_row_bcast(b_sc, s0 + m - 1, 2 * m) for s0 in range(0, L, 2 * m)]
            ref = ref_rows[0] if len(ref_rows) == 1 else jnp.concatenate(ref_rows, axis=0)
            upper = ((rowv >> int(math.log2(m))) & 1) == 1
            w = jnp.exp2(jnp.where(upper, b - ref, ref - b))
            qd = jnp.where(upper, qf * w, 0.0).astype(BF16)
            kd = jnp.where(upper, 0.0, kf * w).astype(BF16)
            scores = scores + jnp.where(same_block(2 * m), _dot_nt(qd, kd), 0.0)
            m *= 2

        xs = []
        for s in range(C):
            cs_ = jnp.concatenate([_row_bcast(c_sc, blk0 + s, C) for blk0 in range(0, L, C)], axis=0)
            ks = jnp.concatenate([_row_bcast(k_sc, blk0 + s, C) for blk0 in range(0, L, C)], axis=0)
            xs.append((qf * jnp.exp2(jnp.minimum(b + cs_, ks))).astype(BF16))
        diag = _dot(jnp.concatenate(xs, axis=1), ed_ref[...])
        scores = scores + jnp.where(diag_mask, diag, 0.0)

        st = st_ref[hd]
        o = _dot(scores.astype(BF16), vb) + _dot_nt((qf * jnp.exp2(b)).astype(BF16), st.astype(BF16))

        b_last = _row_bcast(b_sc, L - 1, L)
        kd_end = (kf * jnp.exp2(b_last - b)).astype(BF16)
        st_ref[hd] = st * jnp.exp2(b_last) + _dot(v.T.astype(BF16), kd_end)

        ms = jnp.mean(o * o, axis=-1, keepdims=True)
        gx = src(3)
        o_ref[rs, cs] = o * lax.rsqrt(ms + EPS) * ng_ref[...] * (gx * _sigmoid(gx))
        between(n, n_chunks * HGRN_HEADS)


def _diag_sum_matrix():
    c, d = HGRN_SUB, HGRN_DIM
    rows_s = np.arange(c * d) // d
    cols = np.arange(HGRN_CHUNK) % c
    return jnp.asarray((rows_s[:, None] == cols[None, :]).astype(np.float32), dtype=BF16)


def _proj_hgrn_kernel(x_ref, g_ref, w_ref, cos_ref, sup_ref, sdn_ref, lb_ref, ng_ref, ed_ref, o_ref, ya_ref,
                      st_ref, h_sc, b_all, k_all, c_all):
    @pl.when(pl.program_id(1) == 0)
    def _():
        st_ref[...] = jnp.zeros_like(st_ref)

    x = x_ref[...]
    ms = jnp.mean(x * x, axis=-1, keepdims=True)
    hn = (x * lax.rsqrt(ms + EPS) * g_ref[...]).astype(BF16)

    def project_head(hd):
        for s0 in range(hd * HGRN_WIDTH, (hd + 1) * HGRN_WIDTH, IN_PROJ_SLAB):
            h_sc[:, s0:s0 + IN_PROJ_SLAB] = _dot(hn, w_ref[:, s0:s0 + IN_PROJ_SLAB])

    project_head(0)
    rope_scale = {c0: sc for a, w, sc in _ROPE_COLS for c0 in range(a, a + w, LANES)}
    slabs = list(range(0, D_PROJ, IN_PROJ_SLAB))

    def project_slab(s0):
        y = _dot(hn, w_ref[:, HGRN_COLS + s0:HGRN_COLS + s0 + IN_PROJ_SLAB])
        for c0 in range(s0, s0 + IN_PROJ_SLAB, LANES):
            r = y[:, c0 - s0:c0 - s0 + LANES]
            if c0 in rope_scale:
                r = _rope(r, cos_ref[...], sup_ref[...], sdn_ref[...])
                r = r if rope_scale[c0] == 1.0 else r * rope_scale[c0]
            o_ref[:, c0:c0 + LANES] = r

    def between(n, total):
        if n + 1 < HGRN_HEADS:
            project_head(n + 1)
        for k in range(n * len(slabs) // total, (n + 1) * len(slabs) // total):
            project_slab(slabs[k])

    _hgrn_chunks(h_sc, lb_ref, ng_ref, ed_ref, ya_ref, st_ref, b_all, k_all, c_all, IN_PROJ_TM // HGRN_CHUNK, between)


def _proj_hgrn(h, g, w_bf16, tables, lower_bound, norm_g):
    bsz, seq, _ = h.shape
    tm, L = IN_PROJ_TM, HGRN_CHUNK
    chains = (tm // L) * HGRN_HEADS
    per_chain = pltpu.VMEM((chains, L, HGRN_DIM), F32)
    tile = lambda w: pl.BlockSpec((None, tm, w), lambda b, j: (b, j, 0))
    return pl.pallas_call(
        _proj_hgrn_kernel,
        out_shape=(jax.ShapeDtypeStruct((bsz, seq, D_PROJ), F32), jax.ShapeDtypeStruct((bsz, seq, HGRN_WIDTH), F32)),
        grid=(bsz, seq // tm),
        in_specs=[tile(D_MODEL), _resident((1, D_MODEL)), _resident((D_MODEL, D_IN))]
        + [pl.BlockSpec((tm, LANES), lambda b, j: (j, 0))] * 3
        + [_resident((1, HGRN_WIDTH)), _resident((1, HGRN_DIM)), _resident((HGRN_SUB * HGRN_DIM, L))],
        out_specs=(tile(D_PROJ), tile(HGRN_WIDTH)),
        scratch_shapes=[pltpu.VMEM((HGRN_HEADS, HGRN_DIM, HGRN_DIM), F32), pltpu.VMEM((tm, HGRN_COLS), F32),
                        per_chain, per_chain, per_chain],
        compiler_params=_cparams(("parallel", "arbitrary")),
        name="proj_hgrn",
    )(h, g.reshape(1, D_MODEL), w_bf16, *tables, lower_bound.reshape(1, HGRN_WIDTH), norm_g.reshape(1, HGRN_DIM),
      _diag_sum_matrix())


def _s5_tile(u_refs, wb_ref, wc_ref, ar_ref, ai_ref, alr_ref, ali_ref, d_ref, gw_ref, gb_ref, xs, carry, perm_sc,
             between):
    tm = S5_TILE
    lc = tm // S5_LANES
    half = S5_WIDTH

    for c in range(S5_COLS):
        for tau in range(lc):
            perm_sc[c, tau * S5_LANES:(tau + 1) * S5_LANES, :] = u_refs[c][pl.ds(tau, S5_LANES, stride=lc), :]
        xs[:, c * 2 * half:(c + 1) * 2 * half] = _dot(perm_sc[c].astype(BF16), wb_ref[c])

    def cmul_add(ar, ai, xr, xi, br, bi):
        return ar * xr - ai * xi + br, ar * xi + ai * xr + bi

    for c in range(S5_COLS):
        re = slice(c * 2 * half, c * 2 * half + half)
        im = slice(c * 2 * half + half, (c + 1) * 2 * half)
        ar = jnp.broadcast_to(ar_ref[:, c * half:(c + 1) * half], (S5_LANES, half))
        ai = jnp.broadcast_to(ai_ref[:, c * half:(c + 1) * half], (S5_LANES, half))

        sr = si = jnp.zeros((S5_LANES, half), F32)
        for tau in range(lc):
            rows = slice(tau * S5_LANES, (tau + 1) * S5_LANES)
            sr, si = cmul_add(ar, ai, sr, si, xs[rows, re], xs[rows, im])
            xs[rows, re] = sr
            xs[rows, im] = si
        between(c)

    last = (lc - 1) * S5_LANES
    sub_id = lax.broadcasted_iota(jnp.int32, (S5_LANES, half), 0)
    for c in range(S5_COLS):
        re = slice(c * 2 * half, c * 2 * half + half)
        im = slice(c * 2 * half + half, (c + 1) * 2 * half)
        alr, ali = alr_ref[:, c * half:(c + 1) * half], ali_ref[:, c * half:(c + 1) * half]
        gr, gi = carry[0:1, re], carry[0:1, im]
        g_re = jnp.zeros((S5_LANES, half), F32)
        g_im = jnp.zeros((S5_LANES, half), F32)
        for sub in range(S5_LANES):
            g_re = jnp.where(sub_id == sub, gr, g_re)
            g_im = jnp.where(sub_id == sub, gi, g_im)
            gr, gi = cmul_add(alr, ali, gr, gi, xs[pl.ds(last + sub, 1), re], xs[pl.ds(last + sub, 1), im])
        carry[0:1, re] = gr
        carry[0:1, im] = gi
        ar = jnp.broadcast_to(ar_ref[:, c * half:(c + 1) * half], (S5_LANES, half))
        ai = jnp.broadcast_to(ai_ref[:, c * half:(c + 1) * half], (S5_LANES, half))

        cr, ci = g_re, g_im
        for tau in range(lc):
            rows = slice(tau * S5_LANES, (tau + 1) * S5_LANES)
            cr, ci = cmul_add(ar, ai, cr, ci, 0.0, 0.0)
            xs[rows, re] = xs[rows, re] + cr
            xs[rows, im] = xs[rows, im] + ci
        between(S5_COLS + c)

    ys = []
    for c in range(S5_COLS):
        perm_sc[c] = _dot(xs[:, c * 2 * half:(c + 1) * 2 * half].astype(BF16), wc_ref[c])
        y = jnp.concatenate([perm_sc[c, pl.ds(sub, lc, stride=S5_LANES), :] for sub in range(S5_LANES)], axis=0)
        ys.append(y + d_ref[:, c * LANES:(c + 1) * LANES] * u_refs[c][...])
    y = jnp.concatenate(ys, axis=1)
    y = y * (0.5 * (1.0 + jnp.tanh(math.sqrt(2.0 / math.pi) * (y + 0.044715 * (y * y * y)))))
    return y * _sigmoid(_dot(y.astype(BF16), gw_ref[...]) + gb_ref[...])


def _s5_params(a_re, a_im, b_re, b_im, c_re, c_im, log_step):
    a = lax.complex(a_re.astype(F32), a_im.astype(F32))
    dt = jnp.exp(log_step.astype(F32))[:, None]
    a_bar = jnp.exp(dt * a)
    b_bar = ((a_bar - 1.0) / a)[..., None] * lax.complex(b_re.astype(F32), b_im.astype(F32))
    a_l = a_bar ** (S5_TILE // S5_LANES)
    gpc = S5_GROUPS // S5_COLS
    eye = jnp.eye(gpc, dtype=F32)

    def in_w(t):
        t = t.reshape(S5_COLS, gpc, S5_STATE, S5_GROUP)
        return jnp.einsum('cgpi,gh->cgihp', t, eye).reshape(S5_COLS, gpc * S5_GROUP, gpc * S5_STATE)

    def out_w(t):
        t = t.reshape(S5_COLS, gpc, S5_GROUP, S5_STATE)
        return jnp.einsum('cgip,gh->cgphi', t, eye).reshape(S5_COLS, gpc * S5_STATE, gpc * S5_GROUP)

    wb = jnp.concatenate([in_w(jnp.real(b_bar)), in_w(jnp.imag(b_bar))], axis=2).astype(BF16)
    wc = jnp.concatenate([out_w(c_re.astype(F32)), -out_w(c_im.astype(F32))], axis=1).astype(BF16)
    flat = lambda t: t.reshape(1, S5_GROUPS * S5_STATE)
    return wb, wc, flat(jnp.real(a_bar)), flat(jnp.imag(a_bar)), flat(jnp.real(a_l)), flat(jnp.imag(a_l))


def _s5_merge_kernel(h_ref, g_ref, u0_ref, u1_ref, u2_ref, u3_ref, ya_ref, yb_ref,
                     o1_ref, o2_ref, o3_ref, l1_ref, l2_ref, l3_ref,
                     sb_ref, sc_ref, ar_ref, ai_ref, alr_ref, ali_ref, d_ref, gw_ref, gb_ref,
                     wg_ref, wa_ref, wb_ref, wc_ref, wd_ref, wo_ref, out_ref,
                     xs, carry, perm_sc, hn_sc, acc_sc):
    @pl.when(pl.program_id(1) == 0)
    def _():
        carry[...] = jnp.zeros_like(carry)

    h = h_ref[...]
    ms = jnp.mean(h * h, axis=-1, keepdims=True)
    hn_sc[...] = (h * lax.rsqrt(ms + EPS) * g_ref[...]).astype(BF16)
    l1, l2, l3 = l1_ref[...], l2_ref[...], l3_ref[...]
    mx = jnp.maximum(jnp.maximum(l1, l2), l3)
    e1, e2, e3 = jnp.exp(l1 - mx), jnp.exp(l2 - mx), jnp.exp(l3 - mx)
    tot = e1 + e2 + e3
    yd = ((e1 / tot) * o1_ref[...] + (e2 / tot) * o2_ref[...] + (e3 / tot) * o3_ref[...]).astype(BF16)

    half_w = D_MODEL // 2

    def gate(branch, cols):
        return _sigmoid(_dot(hn_sc[...], wg_ref[:, branch * D_MODEL + cols.start:branch * D_MODEL + cols.stop]))

    def unit(k):
        which, cols = k // 2, slice((k % 2) * half_w, (k % 2 + 1) * half_w)
        if which == 0:
            acc_sc[:, cols] = gate(0, cols) * _dot(ya_ref[...].astype(BF16), wa_ref[:, cols])
        elif which == 1:
            acc_sc[:, cols] += gate(1, cols) * _dot(yb_ref[...].astype(BF16), wb_ref[:, cols])
        elif which == 2:
            acc_sc[:, cols] += gate(3, cols) * _dot(yd, wd_ref[:, cols])
        else:
            out_ref[:, cols] = gate(2, cols)

    yc = _s5_tile((u0_ref, u1_ref, u2_ref, u3_ref), sb_ref, sc_ref, ar_ref, ai_ref, alr_ref, ali_ref, d_ref, gw_ref,
                  gb_ref, xs, carry, perm_sc, unit)
    merged = acc_sc[...] + out_ref[...] * _dot(yc.astype(BF16), wc_ref[...])
    out_ref[...] = h + _dot(merged.astype(BF16), wo_ref[...])


def _s5_merge(h, g, proj, s5_params, s5_d, glu_w, glu_b, ya, yb, dil, wg, wa, wb, wc, wd, wo):
    bsz, seq, _ = h.shape
    tm = S5_TILE
    sb, sc, ar, ai, alr, ali = s5_params
    nstate = S5_GROUPS * S5_STATE
    base = OFF_S5 // LANES
    tile = lambda w: pl.BlockSpec((None, tm, w), lambda b, j: (b, j, 0))
    in_specs = [tile(D_MODEL), _resident((1, D_MODEL))]
    in_specs += [pl.BlockSpec((None, tm, LANES), (lambda c: (lambda b, j: (b, j, base + c)))(c)) for c in range(S5_COLS)]
    in_specs += [tile(HGRN_WIDTH), tile(SWA_Q_WIDTH)] + [tile(DIL_OUT)] * 6
    in_specs += [_resident((S5_COLS, LANES, 2 * S5_WIDTH)), _resident((S5_COLS, 2 * S5_WIDTH, LANES))]
    in_specs += [_resident((1, nstate))] * 4
    in_specs += [_resident((1, S5_WIDTH)), _resident((S5_WIDTH, S5_WIDTH)), _resident((1, S5_WIDTH))]
    in_specs += [_resident(w.shape) for w in (wg, wa, wb, wc, wd, wo)]
    (o1, s1), (o2, s2), (o3, s3) = dil
    return pl.pallas_call(
        _s5_merge_kernel,
        out_shape=jax.ShapeDtypeStruct((bsz, seq, D_MODEL), F32),
        grid=(bsz, seq // tm),
        in_specs=in_specs,
        out_specs=tile(D_MODEL),
        scratch_shapes=[pltpu.VMEM((tm, 2 * nstate), F32), pltpu.VMEM((S5_LANES, 2 * nstate), F32),
                        pltpu.VMEM((S5_COLS, tm, LANES), F32), pltpu.VMEM((tm, D_MODEL), BF16),
                        pltpu.VMEM((tm, D_MODEL), F32)],
        compiler_params=_cparams(("parallel", "arbitrary"), VMEM_LIMIT_S5_MERGE),
        name="s5_merge",
    )(h, g.reshape(1, D_MODEL), proj, proj, proj, proj, ya, yb, o1, o2, o3, s1, s2, s3, sb, sc, ar, ai, alr, ali,
      s5_d.reshape(1, S5_WIDTH), glu_w.astype(BF16), glu_b.reshape(1, S5_WIDTH), wg, wa, wb, wc, wd, wo)


def _ffn_kernel(h_ref, hp_ref, g_ref, wu_ref, cw_ref, cb_ref, wd_ref, fg_ref, o_ref, hn_sc, z_sc, *, final_norm):
    i = pl.program_id(1)
    tm, tf, halo = FFN_TM, FFN_TF, FFN_HALO

    def norm(x):
        ms = jnp.mean(x * x, axis=-1, keepdims=True)
        return x * lax.rsqrt(ms + EPS) * g_ref[...]

    def stage(t):
        hn_sc[t, halo:, :] = norm(h_ref[t * tm:(t + 1) * tm, :]).astype(BF16)
        if t == 0:
            hn_sc[t, :halo, :] = jnp.where(i == 0, 0.0, norm(hp_ref[...])).astype(BF16)
        else:
            hn_sc[t, :halo, :] = norm(h_ref[t * tm - halo:t * tm, :]).astype(BF16)

    def conv(t, cols):
        u = _dot(hn_sc[t], wu_ref[:, cols])
        u1 = pltpu.roll(u, 1, 0)
        u2 = pltpu.roll(u, 2, 0)
        return (cw_ref[0:1, cols] * u2[halo:, :] + cw_ref[1:2, cols] * u1[halo:, :] + cw_ref[2:3, cols] * u[halo:, :]
                + cb_ref[:, cols])

    stage(0)
    for t in range(FFN_SUBTILES):
        for c in range(D_FF // tf):
            a = conv(t, slice(c * tf, (c + 1) * tf))
            b = conv(t, slice(D_FF + c * tf, D_FF + (c + 1) * tf))
            z_sc[t, :, c * tf:(c + 1) * tf] = ((a * _sigmoid(a)) * b).astype(BF16)
            if c == 0 and t + 1 < FFN_SUBTILES:
                stage(t + 1)
        rows = slice(t * tm, (t + 1) * tm)
        y = h_ref[rows, :] + _dot(z_sc[t], wd_ref[...])
        if final_norm:
            ms = jnp.mean(y * y, axis=-1, keepdims=True)
            y = y * lax.rsqrt(ms + EPS) * fg_ref[...]
        o_ref[rows, :] = y


def _ffn(h, g, w_up, conv_w, conv_b, w_down, final_g, final_norm):
    bsz, seq, _ = h.shape
    tm, halo = FFN_TM, FFN_HALO
    rows = FFN_SUBTILES * tm
    return pl.pallas_call(
        functools.partial(_ffn_kernel, final_norm=final_norm),
        out_shape=jax.ShapeDtypeStruct((bsz, seq, D_MODEL), F32),
        grid=(bsz, seq // rows),
        in_specs=[pl.BlockSpec((None, rows, D_MODEL), lambda b, i: (b, i, 0)),
                  pl.BlockSpec((None, halo, D_MODEL), lambda b, i: (b, jnp.maximum(i * (rows // halo) - 1, 0), 0)),
                  _resident((1, D_MODEL)), _resident((D_MODEL, 2 * D_FF)), _resident((CONV_WIDTH, 2 * D_FF)),
                  _resident((1, 2 * D_FF)), _resident((D_FF, D_MODEL)), _resident((1, D_MODEL))],
        out_specs=pl.BlockSpec((None, rows, D_MODEL), lambda b, i: (b, i, 0)),
        scratch_shapes=[pltpu.VMEM((FFN_SUBTILES, tm + halo, D_MODEL), BF16),
                        pltpu.VMEM((FFN_SUBTILES, tm, D_FF), BF16)],
        compiler_params=_cparams(("parallel", "arbitrary")),
        name="conv_ffn",
    )(h, h, g.reshape(1, D_MODEL), w_up, conv_w, conv_b.reshape(1, -1), w_down, final_g.reshape(1, D_MODEL))


def kernel(x, norm1_g, w_in, hgrn_lb_logits, hgrn_norm_g, attn_sinks, s5_a_re, s5_a_im, s5_b_re, s5_b_im, s5_c_re, s5_c_im, s5_d, s5_log_step, s5_glu_w, s5_glu_b, w_branch_a, w_branch_b, w_branch_c, w_branch_d, w_out, norm2_g, ffn_w_up, ffn_conv_w, ffn_conv_b, ffn_w_down, final_norm_g):
    bsz, seq, _ = x.shape
    tables = _rope_tables(seq)
    p = jax.nn.softmax(hgrn_lb_logits.astype(F32), axis=0)
    lower_bounds = jnp.cumsum(p, axis=0) - p[0]

    h = x
    for l in range(DEPTH):
        w_in_l = _permute_columns(w_in[l]).astype(BF16)
        proj, y_a = _proj_hgrn(h, norm1_g[l], w_in_l, tables, lower_bounds[l], hgrn_norm_g[l])
        y_b = _band_attn(proj, OFF_SWA_Q, OFF_SWA_K, OFF_SWA_V, SWA_Q_WIDTH, SWA_KV_WIDTH, dil=1, nq=8,
                         max_dist=SWA_WINDOW - 1, sinks=attn_sinks[l].astype(F32))
        dil = [_band_attn(proj, OFF_DIL_Q + g * DIL_OUT, OFF_DIL_K + g * DIL_OUT, OFF_DIL_V + g * DIL_OUT, DIL_OUT,
                          DIL_OUT, dil=dilation, nq=DIL_NQ[g], max_dist=window // dilation)
               for g, (window, dilation) in enumerate(DIL_PAIRS)]
        s5_params = _s5_params(s5_a_re[l], s5_a_im[l], s5_b_re[l], s5_b_im[l], s5_c_re[l], s5_c_im[l], s5_log_step[l])
        h = _s5_merge(h, norm1_g[l], proj, s5_params, s5_d[l], s5_glu_w[l], s5_glu_b[l], y_a, y_b, dil,
                      w_in[l][:, REF_GATE_OFF:].astype(BF16), w_branch_a[l].astype(BF16), w_branch_b[l].astype(BF16),
                      w_branch_c[l].astype(BF16), w_branch_d[l].astype(BF16), w_out[l].astype(BF16))
        h = _ffn(h, norm2_g[l], ffn_w_up[l].astype(BF16), ffn_conv_w[l], ffn_conv_b[l], ffn_w_down[l].astype(BF16),
                 final_norm_g, final_norm=(l == DEPTH - 1))
    return h
```

```python
import functools
import math

import numpy as np
import jax
import jax.numpy as jnp
from jax import lax
from jax.experimental import pallas as pl
from jax.experimental.pallas import tpu as pltpu

F32 = jnp.float32
BF16 = jnp.bfloat16

D_MODEL = 1024
DEPTH = 2
HEAD_DIM = 64
BLOCK = 128
ROPE_THETA = 500000.0
ROT_DIM = HEAD_DIM // 4
EPS = 1e-6
MASK_VALUE = -1e30
LB_FLOOR = 1e-30
N_BRANCH = 4

HGRN_HEADS = 4
HGRN_DIM = 128
HGRN_WIDTH = HGRN_HEADS * HGRN_DIM
HGRN_CHUNK = 128
HGRN_SUB = 8

SWA_Q_HEADS = 8
SWA_KV_HEADS = 2
SWA_WINDOW = 128
SWA_Q_WIDTH = SWA_Q_HEADS * HEAD_DIM
SWA_KV_WIDTH = SWA_KV_HEADS * HEAD_DIM

S5_WIDTH = 512
S5_GROUP = 16
S5_GROUPS = S5_WIDTH // S5_GROUP
S5_STATE = 64
S5_COLS = 4
S5_TILE = 512
S5_LANES = 8

DIL_PAIRS = ((128, 1), (512, 4), (2048, 16))
DIL_HEADS_PER_GROUP = 4
DIL_HEADS = DIL_HEADS_PER_GROUP * len(DIL_PAIRS)
DIL_OUT = DIL_HEADS_PER_GROUP * HEAD_DIM
DIL_NQ = (8, 2, 1)

D_FF = 2816
CONV_WIDTH = 3
FFN_TF = 256
FFN_TM = 512
FFN_SUBTILES = 2
FFN_HALO = 16

LANES = 128

HGRN_COLS = 4 * HGRN_WIDTH
OFF_SWA_Q = 0
OFF_SWA_K = OFF_SWA_Q + SWA_Q_WIDTH
OFF_SWA_V = OFF_SWA_K + SWA_KV_WIDTH
OFF_S5 = OFF_SWA_V + SWA_KV_WIDTH
OFF_DIL_Q = OFF_S5 + S5_WIDTH
OFF_DIL_K = OFF_DIL_Q + DIL_HEADS * HEAD_DIM
OFF_DIL_V = OFF_DIL_K + DIL_HEADS * HEAD_DIM
D_PROJ = OFF_DIL_V + DIL_HEADS * HEAD_DIM
D_IN = HGRN_COLS + D_PROJ

VMEM_LIMIT = 56 * 1024 * 1024
VMEM_LIMIT_S5_MERGE = 60 * 1024 * 1024


def _cparams(sem, vmem_limit=VMEM_LIMIT):
    return pltpu.CompilerParams(dimension_semantics=sem, vmem_limit_bytes=vmem_limit)


def _resident(shape):
    return pl.BlockSpec(shape, lambda *_: (0,) * len(shape), pipeline_mode=pl.Buffered(1))


def _sigmoid(x):
    return 0.5 * jnp.tanh(0.5 * x) + 0.5


def _row_bcast(ref, r, n):
    return jnp.broadcast_to(ref[pl.ds(r, 1), :], (n, ref.shape[1]))


def _dot(a, b):
    return jnp.dot(a, b, preferred_element_type=F32)


def _dot_nt(a, b):
    return lax.dot_general(a, b, (((1,), (1,)), ((), ())), preferred_element_type=F32)


_ROPE_COLS = ((OFF_SWA_Q, SWA_Q_WIDTH, HEAD_DIM ** -0.5), (OFF_DIL_Q, DIL_HEADS * HEAD_DIM, HEAD_DIM ** -0.5),
              (OFF_DIL_K, DIL_HEADS * HEAD_DIM, 1.0), (OFF_SWA_K, SWA_KV_WIDTH, 1.0))


IN_PROJ_SLAB = 256
IN_PROJ_TM = 512


def _rope_tables(seq):
    half = ROT_DIM // 2
    inv_freq = ROPE_THETA ** (-jnp.arange(0, ROT_DIM, 2, dtype=F32) / ROT_DIM)
    dim = np.arange(LANES) % HEAD_DIM
    ang = jnp.arange(seq, dtype=F32)[:, None] * inv_freq[dim % half][None, :]
    cos, sin = jnp.cos(ang), jnp.sin(ang)
    cos_t = jnp.where(dim < ROT_DIM, cos, 1.0)
    sup_t = jnp.where(dim < half, -sin, 0.0)
    sdn_t = jnp.where((dim >= half) & (dim < ROT_DIM), sin, 0.0)
    return cos_t, sup_t, sdn_t


def _rope(x, cos, sup, sdn):
    half = ROT_DIM // 2
    return x * cos + pltpu.roll(x, LANES - half, 1) * sup + pltpu.roll(x, half, 1) * sdn


def _band_attn_kernel(*refs, dil, nq, max_dist, has_sink, gqa, pairs):
    if has_sink:
        sink_ref, refs = refs[0], refs[1:]
    q_ref, kc_ref, kp_ref, vc_ref, vp_ref, o_ref = refs[:6]
    refs = refs[6:]
    lse_ref = None
    if not has_sink:
        lse_ref, refs = refs[0], refs[1:]
    j = pl.program_id(1)
    pair0 = pl.program_id(2) * pairs
    blk = BLOCK * dil
    span = nq * blk

    lane = lax.broadcasted_iota(jnp.int32, (1, LANES), 1)
    lo_half = lane < HEAD_DIM

    if gqa:
        k_sc, v_sc = refs
        for sc, prev_ref, cur_ref in ((k_sc, kp_ref, kc_ref), (v_sc, vp_ref, vc_ref)):
            for src, base, n in ((prev_ref, 0, blk), (cur_ref, blk, span)):
                for r0 in range(0, n, min(n, 2 * BLOCK)):
                    rs = slice(r0, r0 + min(n, 2 * BLOCK))
                    t = src[rs, :]
                    sw = pltpu.roll(t, HEAD_DIM, 1)
                    sc[0, base + r0:base + rs.stop, :] = jnp.where(lo_half, t, sw)
                    sc[1, base + r0:base + rs.stop, :] = jnp.where(lo_half, sw, t)

    per_kv = SWA_Q_HEADS // (2 * SWA_KV_HEADS) if gqa else 1
    stack = 2 * per_kv
    qi = lax.broadcasted_iota(jnp.int32, (stack * BLOCK, 2 * BLOCK), 0) & (BLOCK - 1)
    ki = lax.broadcasted_iota(jnp.int32, (stack * BLOCK, 2 * BLOCK), 1)
    dist = qi + BLOCK - ki
    band = (dist >= 0) & (dist <= max_dist)
    head_of_row = lax.broadcasted_iota(jnp.int32, (stack * BLOCK, 1), 0) >> int(math.log2(BLOCK))

    def rows(start, size):
        return pl.ds(start, size) if dil == 1 else pl.ds(start, size, stride=dil)

    def load_kv(cur_ref, prev_ref, sc, i, res, p):
        if gqa:
            return sc[p // (SWA_Q_HEADS // (2 * SWA_KV_HEADS)), i * blk:i * blk + 2 * BLOCK, :]
        cols = slice(p * LANES, (p + 1) * LANES)
        prev = prev_ref[rows(res, BLOCK), cols] if i == 0 else cur_ref[rows((i - 1) * blk + res, BLOCK), cols]
        return jnp.concatenate([prev, cur_ref[rows(i * blk + res, BLOCK), cols]], axis=0)

    for res in range(dil):
        for i in range(nq):
            valid = band & (ki >= jnp.where(j == 0, BLOCK, 0)) if i == 0 else band
            q_rows = rows(i * blk + res, BLOCK)
            for p0 in range(0, pairs, per_kv):
                kp = load_kv(kc_ref, kp_ref, k_sc if gqa else None, i, res, p0).astype(BF16)
                vp = load_kv(vc_ref, vp_ref, v_sc if gqa else None, i, res, p0).astype(BF16)
                qs = []
                for p in range(p0, p0 + per_kv):
                    qp = q_ref[q_rows, p * LANES:(p + 1) * LANES]
                    qs += [jnp.where(lo_half, qp, 0.0), jnp.where(lo_half, 0.0, qp)]
                s = jnp.where(valid, _dot_nt(jnp.concatenate(qs, axis=0).astype(BF16), kp), MASK_VALUE)
                m = jnp.max(s, axis=-1, keepdims=True)
                if has_sink:
                    sk = sink_ref[2 * (pair0 + p0)]
                    for hh in range(1, stack):
                        sk = jnp.where(head_of_row == hh, sink_ref[2 * (pair0 + p0) + hh], sk)
                    m = jnp.maximum(m, sk)
                pe = jnp.exp(s - m)
                den = jnp.sum(pe, axis=-1, keepdims=True)
                if has_sink:
                    den = den + jnp.exp(sk - m)
                o = _dot(pe.astype(BF16), vp) / den
                lse = m + jnp.log(den)
                for p in range(p0, p0 + per_kv):
                    r0 = 2 * (p - p0) * BLOCK
                    cols = slice(p * LANES, (p + 1) * LANES)
                    o_ref[q_rows, cols] = jnp.where(lo_half, o[r0:r0 + BLOCK], o[r0 + BLOCK:r0 + 2 * BLOCK])
                    if lse_ref is not None:
                        lse_ref[q_rows, cols] = jnp.where(lo_half, lse[r0:r0 + BLOCK], lse[r0 + BLOCK:r0 + 2 * BLOCK])


def _band_attn(proj, q_off, k_off, v_off, width, kv_width, dil, nq, max_dist, sinks=None):
    bsz, seq, _ = proj.shape
    blk = BLOCK * dil
    span = nq * blk
    gqa = kv_width != width
    n_pairs = width // LANES
    pairs = n_pairs if dil == 1 else 1
    assert not gqa or (dil == 1 and kv_width == LANES)
    qw = pairs * LANES
    kw = kv_width if gqa else qw
    cur = lambda off, w: (lambda b, j, p: (b, j, off // w + p))
    prev = lambda off, w: (lambda b, j, p: (b, jnp.maximum(j * nq - 1, 0), off // w + p))
    in_specs = [pl.BlockSpec((None, span, qw), cur(q_off, qw)),
                pl.BlockSpec((None, span, kw), cur(k_off, kw)), pl.BlockSpec((None, blk, kw), prev(k_off, kw)),
                pl.BlockSpec((None, span, kw), cur(v_off, kw)), pl.BlockSpec((None, blk, kw), prev(v_off, kw))]
    args = [proj] * 5
    has_sink = sinks is not None
    if has_sink:
        in_specs = [pl.BlockSpec(memory_space=pltpu.SMEM)] + in_specs
        args = [sinks] + args
    o_spec = pl.BlockSpec((None, span, qw), lambda b, j, p: (b, j, p))
    o_shape = jax.ShapeDtypeStruct((bsz, seq, width), F32)
    out_shape, out_specs = (o_shape, o_spec) if has_sink else ((o_shape, o_shape), (o_spec, o_spec))
    return pl.pallas_call(
        functools.partial(_band_attn_kernel, dil=dil, nq=nq, max_dist=max_dist, has_sink=has_sink, gqa=gqa, pairs=pairs),
        out_shape=out_shape,
        grid=(bsz, seq // span, n_pairs // pairs),
        in_specs=in_specs,
        out_specs=out_specs,
        scratch_shapes=[pltpu.VMEM((SWA_KV_HEADS, span + blk, LANES), F32)] * 2 if gqa else [],
        compiler_params=_cparams(("parallel", "arbitrary", "arbitrary")),
        name="swa_attn" if has_sink else f"dil_attn_{dil}",
    )(*args)


def _hgrn_chunks(src_ref, lb_ref, ng_ref, ed_ref, o_ref, st_ref, b_all, k_all, c_all, n_chunks, between):
    L, C, D = HGRN_CHUNK, HGRN_SUB, HGRN_DIM

    row = lax.broadcasted_iota(jnp.int32, (L, L), 0)
    col = lax.broadcasted_iota(jnp.int32, (L, L), 1)
    rowv = lax.broadcasted_iota(jnp.int32, (L, D), 0)
    same_block = lambda size: (row >> int(math.log2(size))) == (col >> int(math.log2(size)))
    diag_mask = same_block(C) & (col <= row)

    for ck, hd in [(ck, hd) for ck in range(n_chunks) for hd in range(HGRN_HEADS)]:
        cs = slice(hd * D, (hd + 1) * D)
        rs = slice(ck * L, (ck + 1) * L)
        src = lambda kind, rs=rs, hd=hd: src_ref[rs, kind * HGRN_WIDTH + hd * D:kind * HGRN_WIDTH + (hd + 1) * D]
        n = ck * HGRN_HEADS + hd
        b_sc, k_sc, c_sc = b_all.at[n], k_all.at[n], c_all.at[n]
        x = src(1)
        lb = lb_ref[:, cs]
        e = jnp.exp(-jnp.abs(x))
        log_sig = jnp.minimum(x, 0.0) - jnp.log(1.0 + e)
        t0 = jnp.log(jnp.maximum(lb, LB_FLOOR))
        t1 = jnp.log1p(-lb) + log_sig
        log_f = jnp.maximum(t0, t1) + jnp.log(1.0 + jnp.exp(-jnp.abs(t0 - t1)))
        kf = (1.0 - lb) * (jnp.where(x > 0.0, e, 1.0) / (1.0 + e))
        qx = src(0)
        qf = qx * _sigmoid(qx)
        v = src(2)
        vb = v.astype(BF16)

        b = log_f * math.log2(math.e)
        shift = 1
        while shift < L:
            b = b + jnp.where(rowv >= shift, pltpu.roll(b, shift, 0), 0.0)
            shift *= 2
        b_sc[...] = b
        lk = jnp.log2(kf)
        k_sc[...] = lk
        c_sc[...] = lk - b

        scores = jnp.zeros((L, L), F32)
        m = C
        while m < L:
            ref_rows = [_row_bcast(b_sc, s0 + m - 1, 2 * m) for s0 in range(0, L, 2 * m)]
            ref = ref_rows[0] if len(ref_rows) == 1 else jnp.concatenate(ref_rows, axis=0)
            upper = ((rowv >> int(math.log2(m))) & 1) == 1
            w = jnp.exp2(jnp.where(upper, b - ref, ref - b))
            qd = jnp.where(upper, qf * w, 0.0).astype(BF16)
            kd = jnp.where(upper, 0.0, kf * w).astype(BF16)
            scores = scores + jnp.where(same_block(2 * m), _dot_nt(qd, kd), 0.0)
            m *= 2

        xs = []
        for s in range(C):
            cs_ = jnp.concatenate([_row_bcast(c_sc, blk0 + s, C) for blk0 in range(0, L, C)], axis=0)
            ks = jnp.concatenate([_row_bcast(k_sc, blk0 + s, C) for blk0 in range(0, L, C)], axis=0)
            xs.append((qf * jnp.exp2(jnp.minimum(b + cs_, ks))).astype(BF16))
        diag = _dot(jnp.concatenate(xs, axis=1), ed_ref[...])
        scores = scores + jnp.where(diag_mask, diag, 0.0)

        st = st_ref[hd]
        o = _dot(scores.astype(BF16), vb) + _dot_nt((qf * jnp.exp2(b)).astype(BF16), st.astype(BF16))

        b_last = _row_bcast(b_sc, L - 1, L)
        kd_end = (kf * jnp.exp2(b_last - b)).astype(BF16)
        st_ref[hd] = st * jnp.exp2(b_last) + _dot(v.T.astype(BF16), kd_end)

        ms = jnp.mean(o * o, axis=-1, keepdims=True)
        gx = src(3)
        o_ref[rs, cs] = o * lax.rsqrt(ms + EPS) * ng_ref[...] * (gx * _sigmoid(gx))
        between(n, n_chunks * HGRN_HEADS)


def _diag_sum_matrix():
    c, d = HGRN_SUB, HGRN_DIM
    rows_s = np.arange(c * d) // d
    cols = np.arange(HGRN_CHUNK) % c
    return jnp.asarray((rows_s[:, None] == cols[None, :]).astype(np.float32), dtype=BF16)


def _proj_hgrn_kernel(x_ref, g_ref, w_ref, cos_ref, sup_ref, sdn_ref, lb_ref, ng_ref, ed_ref, o_ref, ya_ref,
                      st_ref, h_sc, b_all, k_all, c_all):
    @pl.when(pl.program_id(1) == 0)
    def _():
        st_ref[...] = jnp.zeros_like(st_ref)

    x = x_ref[...]
    ms = jnp.mean(x * x, axis=-1, keepdims=True)
    hn = (x * lax.rsqrt(ms + EPS) * g_ref[...]).astype(BF16)
    for s0 in range(0, HGRN_COLS, IN_PROJ_SLAB):
        h_sc[:, s0:s0 + IN_PROJ_SLAB] = _dot(hn, w_ref[:, s0:s0 + IN_PROJ_SLAB])

    rope_scale = {c0: sc for a, w, sc in _ROPE_COLS for c0 in range(a, a + w, LANES)}
    slabs = list(range(0, D_PROJ, IN_PROJ_SLAB))

    def project_slab(s0):
        y = _dot(hn, w_ref[:, HGRN_COLS + s0:HGRN_COLS + s0 + IN_PROJ_SLAB])
        for c0 in range(s0, s0 + IN_PROJ_SLAB, LANES):
            r = y[:, c0 - s0:c0 - s0 + LANES]
            if c0 in rope_scale:
                r = _rope(r, cos_ref[...], sup_ref[...], sdn_ref[...])
                r = r if rope_scale[c0] == 1.0 else r * rope_scale[c0]
            o_ref[:, c0:c0 + LANES] = r

    def between(n, total):
        for k in range(n * len(slabs) // total, (n + 1) * len(slabs) // total):
            project_slab(slabs[k])

    _hgrn_chunks(h_sc, lb_ref, ng_ref, ed_ref, ya_ref, st_ref, b_all, k_all, c_all, IN_PROJ_TM // HGRN_CHUNK, between)


def _proj_hgrn(h, g, w_bf16, tables, lower_bound, norm_g):
    bsz, seq, _ = h.shape
    tm, L = IN_PROJ_TM, HGRN_CHUNK
    chains = (tm // L) * HGRN_HEADS
    per_chain = pltpu.VMEM((chains, L, HGRN_DIM), F32)
    tile = lambda w: pl.BlockSpec((None, tm, w), lambda b, j: (b, j, 0))
    return pl.pallas_call(
        _proj_hgrn_kernel,
        out_shape=(jax.ShapeDtypeStruct((bsz, seq, D_PROJ), F32), jax.ShapeDtypeStruct((bsz, seq, HGRN_WIDTH), F32)),
        grid=(bsz, seq // tm),
        in_specs=[tile(D_MODEL), _resident((1, D_MODEL)), _resident((D_MODEL, D_IN))]
        + [pl.BlockSpec((tm, LANES), lambda b, j: (j, 0))] * 3
        + [_resident((1, HGRN_WIDTH)), _resident((1, HGRN_DIM)), _resident((HGRN_SUB * HGRN_DIM, L))],
        out_specs=(tile(D_PROJ), tile(HGRN_WIDTH)),
        scratch_shapes=[pltpu.VMEM((HGRN_HEADS, HGRN_DIM, HGRN_DIM), F32), pltpu.VMEM((tm, HGRN_COLS), F32),
                        per_chain, per_chain, per_chain],
        compiler_params=_cparams(("parallel", "arbitrary")),
        name="proj_hgrn",
    )(h, g.reshape(1, D_MODEL), w_bf16, *tables, lower_bound.reshape(1, HGRN_WIDTH), norm_g.reshape(1, HGRN_DIM),
      _diag_sum_matrix())


def _s5_tile(u_refs, wb_ref, wc_ref, ar_ref, ai_ref, alr_ref, ali_ref, d_ref, gw_ref, gb_ref, xs, carry, perm_sc,
             between):
    tm = S5_TILE
    lc = tm // S5_LANES
    half = S5_WIDTH

    for c in range(S5_COLS):
        for tau in range(lc):
            perm_sc[c, tau * S5_LANES:(tau + 1) * S5_LANES, :] = u_refs[c][pl.ds(tau, S5_LANES, stride=lc), :]
        xs[:, c * 2 * half:(c + 1) * 2 * half] = _dot(perm_sc[c].astype(BF16), wb_ref[c])

    def cmul_add(ar, ai, xr, xi, br, bi):
        return ar * xr - ai * xi + br, ar * xi + ai * xr + bi

    for c in range(S5_COLS):
        re = slice(c * 2 * half, c * 2 * half + half)
        im = slice(c * 2 * half + half, (c + 1) * 2 * half)
        ar = jnp.broadcast_to(ar_ref[:, c * half:(c + 1) * half], (S5_LANES, half))
        ai = jnp.broadcast_to(ai_ref[:, c * half:(c + 1) * half], (S5_LANES, half))

        sr = si = jnp.zeros((S5_LANES, half), F32)
        for tau in range(lc):
            rows = slice(tau * S5_LANES, (tau + 1) * S5_LANES)
            sr, si = cmul_add(ar, ai, sr, si, xs[rows, re], xs[rows, im])
            xs[rows, re] = sr
            xs[rows, im] = si
        between(c)

    last = (lc - 1) * S5_LANES
    sub_id = lax.broadcasted_iota(jnp.int32, (S5_LANES, half), 0)
    for c in range(S5_COLS):
        re = slice(c * 2 * half, c * 2 * half + half)
        im = slice(c * 2 * half + half, (c + 1) * 2 * half)
        alr, ali = alr_ref[:, c * half:(c + 1) * half], ali_ref[:, c * half:(c + 1) * half]
        gr, gi = carry[0:1, re], carry[0:1, im]
        g_re = jnp.zeros((S5_LANES, half), F32)
        g_im = jnp.zeros((S5_LANES, half), F32)
        for sub in range(S5_LANES):
            g_re = jnp.where(sub_id == sub, gr, g_re)
            g_im = jnp.where(sub_id == sub, gi, g_im)
            gr, gi = cmul_add(alr, ali, gr, gi, xs[pl.ds(last + sub, 1), re], xs[pl.ds(last + sub, 1), im])
        carry[0:1, re] = gr
        carry[0:1, im] = gi
        ar = jnp.broadcast_to(ar_ref[:, c * half:(c + 1) * half], (S5_LANES, half))
        ai = jnp.broadcast_to(ai_ref[:, c * half:(c + 1) * half], (S5_LANES, half))

        cr, ci = g_re, g_im
        for tau in range(lc):
            rows = slice(tau * S5_LANES, (tau + 1) * S5_LANES)
            cr, ci = cmul_add(ar, ai, cr, ci, 0.0, 0.0)
            xs[rows, re] = xs[rows, re] + cr
            xs[rows, im] = xs[rows, im] + ci
        between(S5_COLS + c)

    ys = []
    for c in range(S5_COLS):
        perm_sc[c] = _dot(xs[:, c * 2 * half:(c + 1) * 2 * half].astype(BF16), wc_ref[c])
        y = jnp.concatenate([perm_sc[c, pl.ds(sub, lc, stride=S5_LANES), :] for sub in range(S5_LANES)], axis=0)
        ys.append(y + d_ref[:, c * LANES:(c + 1) * LANES] * u_refs[c][...])
    y = jnp.concatenate(ys, axis=1)
    y = y * (0.5 * (1.0 + jnp.tanh(math.sqrt(2.0 / math.pi) * (y + 0.044715 * (y * y * y)))))
    return y * _sigmoid(_dot(y.astype(BF16), gw_ref[...]) + gb_ref[...])


def _s5_params(a_re, a_im, b_re, b_im, c_re, c_im, log_step):
    a = lax.complex(a_re.astype(F32), a_im.astype(F32))
    dt = jnp.exp(log_step.astype(F32))[:, None]
    a_bar = jnp.exp(dt * a)
    b_bar = ((a_bar - 1.0) / a)[..., None] * lax.complex(b_re.astype(F32), b_im.astype(F32))
    a_l = a_bar ** (S5_TILE // S5_LANES)
    gpc = S5_GROUPS // S5_COLS
    eye = jnp.eye(gpc, dtype=F32)

    def in_w(t):
        t = t.reshape(S5_COLS, gpc, S5_STATE, S5_GROUP)
        return jnp.einsum('cgpi,gh->cgihp', t, eye).reshape(S5_COLS, gpc * S5_GROUP, gpc * S5_STATE)

    def out_w(t):
        t = t.reshape(S5_COLS, gpc, S5_GROUP, S5_STATE)
        return jnp.einsum('cgip,gh->cgphi', t, eye).reshape(S5_COLS, gpc * S5_STATE, gpc * S5_GROUP)

    wb = jnp.concatenate([in_w(jnp.real(b_bar)), in_w(jnp.imag(b_bar))], axis=2).astype(BF16)
    wc = jnp.concatenate([out_w(c_re.astype(F32)), -out_w(c_im.astype(F32))], axis=1).astype(BF16)
    flat = lambda t: t.reshape(1, S5_GROUPS * S5_STATE)
    return wb, wc, flat(jnp.real(a_bar)), flat(jnp.imag(a_bar)), flat(jnp.real(a_l)), flat(jnp.imag(a_l))


def _s5_merge_kernel(h_ref, g_ref, u0_ref, u1_ref, u2_ref, u3_ref, ya_ref, yb_ref,
                     o1_ref, o2_ref, o3_ref, l1_ref, l2_ref, l3_ref,
                     sb_ref, sc_ref, ar_ref, ai_ref, alr_ref, ali_ref, d_ref, gw_ref, gb_ref,
                     wg_ref, wa_ref, wb_ref, wc_ref, wd_ref, wo_ref, out_ref,
                     xs, carry, perm_sc, hn_sc, acc_sc):
    @pl.when(pl.program_id(1) == 0)
    def _():
        carry[...] = jnp.zeros_like(carry)

    h = h_ref[...]
    ms = jnp.mean(h * h, axis=-1, keepdims=True)
    hn_sc[...] = (h * lax.rsqrt(ms + EPS) * g_ref[...]).astype(BF16)
    l1, l2, l3 = l1_ref[...], l2_ref[...], l3_ref[...]
    mx = jnp.maximum(jnp.maximum(l1, l2), l3)
    e1, e2, e3 = jnp.exp(l1 - mx), jnp.exp(l2 - mx), jnp.exp(l3 - mx)
    tot = e1 + e2 + e3
    yd = ((e1 / tot) * o1_ref[...] + (e2 / tot) * o2_ref[...] + (e3 / tot) * o3_ref[...]).astype(BF16)

    half_w = D_MODEL // 2

    def gate(branch, cols):
        return _sigmoid(_dot(hn_sc[...], wg_ref[:, branch * D_MODEL + cols.start:branch * D_MODEL + cols.stop]))

    def unit(k):
        which, cols = k // 2, slice((k % 2) * half_w, (k % 2 + 1) * half_w)
        if which == 0:
            acc_sc[:, cols] = gate(0, cols) * _dot(ya_ref[...].astype(BF16), wa_ref[:, cols])
        elif which == 1:
            acc_sc[:, cols] += gate(1, cols) * _dot(yb_ref[...].astype(BF16), wb_ref[:, cols])
        elif which == 2:
            acc_sc[:, cols] += gate(3, cols) * _dot(yd, wd_ref[:, cols])
        else:
            out_ref[:, cols] = gate(2, cols)

    yc = _s5_tile((u0_ref, u1_ref, u2_ref, u3_ref), sb_ref, sc_ref, ar_ref, ai_ref, alr_ref, ali_ref, d_ref, gw_ref,
                  gb_ref, xs, carry, perm_sc, unit)
    merged = acc_sc[...] + out_ref[...] * _dot(yc.astype(BF16), wc_ref[...])
    out_ref[...] = h + _dot(merged.astype(BF16), wo_ref[...])


def _s5_merge(h, g, proj, s5_params, s5_d, glu_w, glu_b, ya, yb, dil, wg, wa, wb, wc, wd, wo):
    bsz, seq, _ = h.shape
    tm = S5_TILE
    sb, sc, ar, ai, alr, ali = s5_params
    nstate = S5_GROUPS * S5_STATE
    base = OFF_S5 // LANES
    tile = lambda w: pl.BlockSpec((None, tm, w), lambda b, j: (b, j, 0))
    in_specs = [tile(D_MODEL), _resident((1, D_MODEL))]
    in_specs += [pl.BlockSpec((None, tm, LANES), (lambda c: (lambda b, j: (b, j, base + c)))(c)) for c in range(S5_COLS)]
    in_specs += [tile(HGRN_WIDTH), tile(SWA_Q_WIDTH)] + [tile(DIL_OUT)] * 6
    in_specs += [_resident((S5_COLS, LANES, 2 * S5_WIDTH)), _resident((S5_COLS, 2 * S5_WIDTH, LANES))]
    in_specs += [_resident((1, nstate))] * 4
    in_specs += [_resident((1, S5_WIDTH)), _resident((S5_WIDTH, S5_WIDTH)), _resident((1, S5_WIDTH))]
    in_specs += [_resident(w.shape) for w in (wg, wa, wb, wc, wd, wo)]
    (o1, s1), (o2, s2), (o3, s3) = dil
    return pl.pallas_call(
        _s5_merge_kernel,
        out_shape=jax.ShapeDtypeStruct((bsz, seq, D_MODEL), F32),
        grid=(bsz, seq // tm),
        in_specs=in_specs,
        out_specs=tile(D_MODEL),
        scratch_shapes=[pltpu.VMEM((tm, 2 * nstate), F32), pltpu.VMEM((S5_LANES, 2 * nstate), F32),
                        pltpu.VMEM((S5_COLS, tm, LANES), F32), pltpu.VMEM((tm, D_MODEL), BF16),
                        pltpu.VMEM((tm, D_MODEL), F32)],
        compiler_params=_cparams(("parallel", "arbitrary"), VMEM_LIMIT_S5_MERGE),
        name="s5_merge",
    )(h, g.reshape(1, D_MODEL), proj, proj, proj, proj, ya, yb, o1, o2, o3, s1, s2, s3, sb, sc, ar, ai, alr, ali,
      s5_d.reshape(1, S5_WIDTH), glu_w.astype(BF16), glu_b.reshape(1, S5_WIDTH), wg, wa, wb, wc, wd, wo)


def _ffn_kernel(h_ref, hp_ref, g_ref, wu_ref, cw_ref, cb_ref, wd_ref, fg_ref, o_ref, hn_sc, z_sc, *, final_norm):
    i = pl.program_id(1)
    tm, tf, halo = FFN_TM, FFN_TF, FFN_HALO

    def norm(x):
        ms = jnp.mean(x * x, axis=-1, keepdims=True)
        return x * lax.rsqrt(ms + EPS) * g_ref[...]

    def stage(t):
        hn_sc[t, halo:, :] = norm(h_ref[t * tm:(t + 1) * tm, :]).astype(BF16)
        if t == 0:
            hn_sc[t, :halo, :] = jnp.where(i == 0, 0.0, norm(hp_ref[...])).astype(BF16)
        else:
            hn_sc[t, :halo, :] = norm(h_ref[t * tm - halo:t * tm, :]).astype(BF16)

    def conv(t, cols):
        u = _dot(hn_sc[t], wu_ref[:, cols])
        u1 = pltpu.roll(u, 1, 0)
        u2 = pltpu.roll(u, 2, 0)
        return (cw_ref[0:1, cols] * u2[halo:, :] + cw_ref[1:2, cols] * u1[halo:, :] + cw_ref[2:3, cols] * u[halo:, :]
                + cb_ref[:, cols])

    stage(0)
    for t in range(FFN_SUBTILES):
        for c in range(D_FF // tf):
            a = conv(t, slice(c * tf, (c + 1) * tf))
            b = conv(t, slice(D_FF + c * tf, D_FF + (c + 1) * tf))
            z_sc[t, :, c * tf:(c + 1) * tf] = ((a * _sigmoid(a)) * b).astype(BF16)
            if c == 0 and t + 1 < FFN_SUBTILES:
                stage(t + 1)
        rows = slice(t * tm, (t + 1) * tm)
        y = h_ref[rows, :] + _dot(z_sc[t], wd_ref[...])
        if final_norm:
            ms = jnp.mean(y * y, axis=-1, keepdims=True)
            y = y * lax.rsqrt(ms + EPS) * fg_ref[...]
        o_ref[rows, :] = y


def _ffn(h, g, w_up, conv_w, conv_b, w_down, final_g, final_norm):
    bsz, seq, _ = h.shape
    tm, halo = FFN_TM, FFN_HALO
    rows = FFN_SUBTILES * tm
    return pl.pallas_call(
        functools.partial(_ffn_kernel, final_norm=final_norm),
        out_shape=jax.ShapeDtypeStruct((bsz, seq, D_MODEL), F32),
        grid=(bsz, seq // rows),
        in_specs=[pl.BlockSpec((None, rows, D_MODEL), lambda b, i: (b, i, 0)),
                  pl.BlockSpec((None, halo, D_MODEL), lambda b, i: (b, jnp.maximum(i * (rows // halo) - 1, 0), 0)),
                  _resident((1, D_MODEL)), _resident((D_MODEL, 2 * D_FF)), _resident((CONV_WIDTH, 2 * D_FF)),
                  _resident((1, 2 * D_FF)), _resident((D_FF, D_MODEL)), _resident((1, D_MODEL))],
        out_specs=pl.BlockSpec((None, rows, D_MODEL), lambda b, i: (b, i, 0)),
        scratch_shapes=[pltpu.VMEM((FFN_SUBTILES, tm + halo, D_MODEL), BF16),
                        pltpu.VMEM((FFN_SUBTILES, tm, D_FF), BF16)],
        compiler_params=_cparams(("parallel", "arbitrary")),
        name="conv_ffn",
    )(h, h, g.reshape(1, D_MODEL), w_up, conv_w, conv_b.reshape(1, -1), w_down, final_g.reshape(1, D_MODEL))


def kernel(x, norm1_g, w_in, hgrn_lb_logits, hgrn_norm_g, attn_sinks, s5_a_re, s5_a_im, s5_b_re, s5_b_im, s5_c_re, s5_c_im, s5_d, s5_log_step, s5_glu_w, s5_glu_b, w_branch_a, w_branch_b, w_branch_c, w_branch_d, w_out, norm2_g, ffn_w_up, ffn_conv_w, ffn_conv_b, ffn_w_down, final_norm_g):
    bsz, seq, _ = x.shape
    tables = _rope_tables(seq)
    p = jax.nn.softmax(hgrn_lb_logits.astype(F32), axis=0)
    lower_bounds = jnp.cumsum(p, axis=0) - p[0]

    w_in_bf16 = w_in.astype(BF16)

    h = x
    for l in range(DEPTH):
        proj, y_a = _proj_hgrn(h, norm1_g[l], w_in_bf16[l], tables, lower_bounds[l], hgrn_norm_g[l])
        y_b = _band_attn(proj, OFF_SWA_Q, OFF_SWA_K, OFF_SWA_V, SWA_Q_WIDTH, SWA_KV_WIDTH, dil=1, nq=8,
                         max_dist=SWA_WINDOW - 1, sinks=attn_sinks[l].astype(F32))
        dil = [_band_attn(proj, OFF_DIL_Q + g * DIL_OUT, OFF_DIL_K + g * DIL_OUT, OFF_DIL_V + g * DIL_OUT, DIL_OUT,
                          DIL_OUT, dil=dilation, nq=DIL_NQ[g], max_dist=window // dilation)
               for g, (window, dilation) in enumerate(DIL_PAIRS)]
        s5_params = _s5_params(s5_a_re[l], s5_a_im[l], s5_b_re[l], s5_b_im[l], s5_c_re[l], s5_c_im[l], s5_log_step[l])
        h = _s5_merge(h, norm1_g[l], proj, s5_params, s5_d[l], s5_glu_w[l], s5_glu_b[l], y_a, y_b, dil,
                      w_in_bf16[l][:, D_IN:], w_branch_a[l].astype(BF16), w_branch_b[l].astype(BF16),
                      w_branch_c[l].astype(BF16), w_branch_d[l].astype(BF16), w_out[l].astype(BF16))
        h = _ffn(h, norm2_g[l], ffn_w_up[l].astype(BF16), ffn_conv_w[l], ffn_conv_b[l], ffn_w_down[l].astype(BF16),
                 final_norm_g, final_norm=(l == DEPTH - 1))
    return h
```

```python
import functools
import math

import numpy as np
import jax
import jax.numpy as jnp
from jax import lax
from jax.experimental import pallas as pl
from jax.experimental.pallas import tpu as pltpu

F32 = jnp.float32
BF16 = jnp.bfloat16

D_MODEL = 1024
DEPTH = 2
HEAD_DIM = 64
BLOCK = 128
ROPE_THETA = 500000.0
ROT_DIM = HEAD_DIM // 4
EPS = 1e-6
MASK_VALUE = -1e30
LB_FLOOR = 1e-30
N_BRANCH = 4

HGRN_HEADS = 4
HGRN_DIM = 128
HGRN_WIDTH = HGRN_HEADS * HGRN_DIM
HGRN_CHUNK = 128
HGRN_SUB = 8

SWA_Q_HEADS = 8
SWA_KV_HEADS = 2
SWA_WINDOW = 128
SWA_Q_WIDTH = SWA_Q_HEADS * HEAD_DIM
SWA_KV_WIDTH = SWA_KV_HEADS * HEAD_DIM

S5_WIDTH = 512
S5_GROUP = 16
S5_GROUPS = S5_WIDTH // S5_GROUP
S5_STATE = 64
S5_COLS = 4
S5_TILE = 512
S5_LANES = 8

DIL_PAIRS = ((128, 1), (512, 4), (2048, 16))
DIL_HEADS_PER_GROUP = 4
DIL_HEADS = DIL_HEADS_PER_GROUP * len(DIL_PAIRS)
DIL_OUT = DIL_HEADS_PER_GROUP * HEAD_DIM
DIL_NQ = (16, 4, 1)
SWA_NQ = 16

D_FF = 2816
CONV_WIDTH = 3
FFN_TF = 256
FFN_TM = 512
FFN_SUBTILES = 2
FFN_HALO = 16

LANES = 128

HGRN_COLS = 4 * HGRN_WIDTH
OFF_SWA_Q = 0
OFF_SWA_K = OFF_SWA_Q + SWA_Q_WIDTH
OFF_SWA_V = OFF_SWA_K + SWA_KV_WIDTH
OFF_S5 = OFF_SWA_V + SWA_KV_WIDTH
OFF_DIL_Q = OFF_S5 + S5_WIDTH
OFF_DIL_K = OFF_DIL_Q + DIL_HEADS * HEAD_DIM
OFF_DIL_V = OFF_DIL_K + DIL_HEADS * HEAD_DIM
D_PROJ = OFF_DIL_V + DIL_HEADS * HEAD_DIM
D_IN = HGRN_COLS + D_PROJ

VMEM_LIMIT = 56 * 1024 * 1024
VMEM_LIMIT_S5_MERGE = 60 * 1024 * 1024


def _cparams(sem, vmem_limit=VMEM_LIMIT):
    return pltpu.CompilerParams(dimension_semantics=sem, vmem_limit_bytes=vmem_limit)


def _resident(shape):
    return pl.BlockSpec(shape, lambda *_: (0,) * len(shape), pipeline_mode=pl.Buffered(1))


def _sigmoid(x):
    return 0.5 * jnp.tanh(0.5 * x) + 0.5


def _row_bcast(ref, r, n):
    return jnp.broadcast_to(ref[pl.ds(r, 1), :], (n, ref.shape[1]))


def _dot(a, b):
    return jnp.dot(a, b, preferred_element_type=F32)


def _dot_nt(a, b):
    return lax.dot_general(a, b, (((1,), (1,)), ((), ())), preferred_element_type=F32)


_ROPE_COLS = ((OFF_SWA_Q, SWA_Q_WIDTH, HEAD_DIM ** -0.5), (OFF_DIL_Q, DIL_HEADS * HEAD_DIM, HEAD_DIM ** -0.5),
              (OFF_DIL_K, DIL_HEADS * HEAD_DIM, 1.0), (OFF_SWA_K, SWA_KV_WIDTH, 1.0))


IN_PROJ_SLAB = 256
IN_PROJ_TM = 512


def _rope_tables(seq):
    half = ROT_DIM // 2
    inv_freq = ROPE_THETA ** (-jnp.arange(0, ROT_DIM, 2, dtype=F32) / ROT_DIM)
    dim = np.arange(LANES) % HEAD_DIM
    ang = jnp.arange(seq, dtype=F32)[:, None] * inv_freq[dim % half][None, :]
    cos, sin = jnp.cos(ang), jnp.sin(ang)
    cos_t = jnp.where(dim < ROT_DIM, cos, 1.0)
    sup_t = jnp.where(dim < half, -sin, 0.0)
    sdn_t = jnp.where((dim >= half) & (dim < ROT_DIM), sin, 0.0)
    return cos_t, sup_t, sdn_t


def _rope(x, cos, sup, sdn):
    half = ROT_DIM // 2
    return x * cos + pltpu.roll(x, LANES - half, 1) * sup + pltpu.roll(x, half, 1) * sdn


def _band_attn_kernel(*refs, dil, nq, max_dist, has_sink, gqa, pairs):
    if has_sink:
        sink_ref, refs = refs[0], refs[1:]
    q_ref, kc_ref, kp_ref, vc_ref, vp_ref, o_ref = refs[:6]
    refs = refs[6:]
    lse_ref = None
    if not has_sink:
        lse_ref, refs = refs[0], refs[1:]
    j = pl.program_id(1)
    pair0 = pl.program_id(2) * pairs
    blk = BLOCK * dil
    span = nq * blk

    lane = lax.broadcasted_iota(jnp.int32, (1, LANES), 1)
    lo_half = lane < HEAD_DIM

    if gqa:
        k_sc, v_sc = refs
        for sc, prev_ref, cur_ref in ((k_sc, kp_ref, kc_ref), (v_sc, vp_ref, vc_ref)):
            for src, base, n in ((prev_ref, 0, blk), (cur_ref, blk, span)):
                for r0 in range(0, n, min(n, 2 * BLOCK)):
                    rs = slice(r0, r0 + min(n, 2 * BLOCK))
                    t = src[rs, :]
                    sw = pltpu.roll(t, HEAD_DIM, 1)
                    sc[0, base + r0:base + rs.stop, :] = jnp.where(lo_half, t, sw)
                    sc[1, base + r0:base + rs.stop, :] = jnp.where(lo_half, sw, t)

    per_kv = SWA_Q_HEADS // (2 * SWA_KV_HEADS) if gqa else 1
    stack = 2 * per_kv
    qi = lax.broadcasted_iota(jnp.int32, (stack * BLOCK, 2 * BLOCK), 0) & (BLOCK - 1)
    ki = lax.broadcasted_iota(jnp.int32, (stack * BLOCK, 2 * BLOCK), 1)
    dist = qi + BLOCK - ki
    band = (dist >= 0) & (dist <= max_dist)
    head_of_row = lax.broadcasted_iota(jnp.int32, (stack * BLOCK, 1), 0) >> int(math.log2(BLOCK))

    def rows(start, size):
        return pl.ds(start, size) if dil == 1 else pl.ds(start, size, stride=dil)

    def load_kv(cur_ref, prev_ref, sc, i, res, p):
        if gqa:
            return sc[p // (SWA_Q_HEADS // (2 * SWA_KV_HEADS)), i * blk:i * blk + 2 * BLOCK, :]
        cols = slice(p * LANES, (p + 1) * LANES)
        prev = prev_ref[rows(res, BLOCK), cols] if i == 0 else cur_ref[rows((i - 1) * blk + res, BLOCK), cols]
        return jnp.concatenate([prev, cur_ref[rows(i * blk + res, BLOCK), cols]], axis=0)

    for res in range(dil):
        for i in range(nq):
            valid = band & (ki >= jnp.where(j == 0, BLOCK, 0)) if i == 0 else band
            q_rows = rows(i * blk + res, BLOCK)
            for p0 in range(0, pairs, per_kv):
                kp = load_kv(kc_ref, kp_ref, k_sc if gqa else None, i, res, p0).astype(BF16)
                vp = load_kv(vc_ref, vp_ref, v_sc if gqa else None, i, res, p0).astype(BF16)
                qs = []
                for p in range(p0, p0 + per_kv):
                    qp = q_ref[q_rows, p * LANES:(p + 1) * LANES]
                    qs += [jnp.where(lo_half, qp, 0.0), jnp.where(lo_half, 0.0, qp)]
                s = jnp.where(valid, _dot_nt(jnp.concatenate(qs, axis=0).astype(BF16), kp), MASK_VALUE)
                m = jnp.max(s, axis=-1, keepdims=True)
                if has_sink:
                    sk = sink_ref[2 * (pair0 + p0)]
                    for hh in range(1, stack):
                        sk = jnp.where(head_of_row == hh, sink_ref[2 * (pair0 + p0) + hh], sk)
                    m = jnp.maximum(m, sk)
                pe = jnp.exp(s - m)
                den = jnp.sum(pe, axis=-1, keepdims=True)
                if has_sink:
                    den = den + jnp.exp(sk - m)
                o = _dot(pe.astype(BF16), vp) / den
                lse = m + jnp.log(den)
                for p in range(p0, p0 + per_kv):
                    r0 = 2 * (p - p0) * BLOCK
                    cols = slice(p * LANES, (p + 1) * LANES)
                    o_ref[q_rows, cols] = jnp.where(lo_half, o[r0:r0 + BLOCK], o[r0 + BLOCK:r0 + 2 * BLOCK])
                    if lse_ref is not None:
                        lse_ref[q_rows, cols] = jnp.where(lo_half, lse[r0:r0 + BLOCK], lse[r0 + BLOCK:r0 + 2 * BLOCK])


def _band_attn(proj, q_off, k_off, v_off, width, kv_width, dil, nq, max_dist, sinks=None):
    bsz, seq, _ = proj.shape
    blk = BLOCK * dil
    span = nq * blk
    gqa = kv_width != width
    n_pairs = width // LANES
    pairs = n_pairs if dil == 1 else 1
    assert not gqa or (dil == 1 and kv_width == LANES)
    qw = pairs * LANES
    kw = kv_width if gqa else qw
    cur = lambda off, w: (lambda b, j, p: (b, j, off // w + p))
    prev = lambda off, w: (lambda b, j, p: (b, jnp.maximum(j * nq - 1, 0), off // w + p))
    in_specs = [pl.BlockSpec((None, span, qw), cur(q_off, qw)),
                pl.BlockSpec((None, span, kw), cur(k_off, kw)), pl.BlockSpec((None, blk, kw), prev(k_off, kw)),
                pl.BlockSpec((None, span, kw), cur(v_off, kw)), pl.BlockSpec((None, blk, kw), prev(v_off, kw))]
    args = [proj] * 5
    has_sink = sinks is not None
    if has_sink:
        in_specs = [pl.BlockSpec(memory_space=pltpu.SMEM)] + in_specs
        args = [sinks] + args
    o_spec = pl.BlockSpec((None, span, qw), lambda b, j, p: (b, j, p))
    o_shape = jax.ShapeDtypeStruct((bsz, seq, width), F32)
    out_shape, out_specs = (o_shape, o_spec) if has_sink else ((o_shape, o_shape), (o_spec, o_spec))
    return pl.pallas_call(
        functools.partial(_band_attn_kernel, dil=dil, nq=nq, max_dist=max_dist, has_sink=has_sink, gqa=gqa, pairs=pairs),
        out_shape=out_shape,
        grid=(bsz, seq // span, n_pairs // pairs),
        in_specs=in_specs,
        out_specs=out_specs,
        scratch_shapes=[pltpu.VMEM((SWA_KV_HEADS, span + blk, LANES), F32)] * 2 if gqa else [],
        compiler_params=_cparams(("parallel", "arbitrary", "arbitrary")),
        name="swa_attn" if has_sink else f"dil_attn_{dil}",
    )(*args)


def _hgrn_chunks(src_ref, lb_ref, ng_ref, ed_ref, o_ref, st_ref, b_all, k_all, c_all, n_chunks, between):
    L, C, D = HGRN_CHUNK, HGRN_SUB, HGRN_DIM

    row = lax.broadcasted_iota(jnp.int32, (L, L), 0)
    col = lax.broadcasted_iota(jnp.int32, (L, L), 1)
    rowv = lax.broadcasted_iota(jnp.int32, (L, D), 0)
    same_block = lambda size: (row >> int(math.log2(size))) == (col >> int(math.log2(size)))
    diag_mask = same_block(C) & (col <= row)

    for ck, hd in [(ck, hd) for ck in range(n_chunks) for hd in range(HGRN_HEADS)]:
        cs = slice(hd * D, (hd + 1) * D)
        rs = slice(ck * L, (ck + 1) * L)
        src = lambda kind, rs=rs, hd=hd: src_ref[rs, kind * HGRN_WIDTH + hd * D:kind * HGRN_WIDTH + (hd + 1) * D]
        n = ck * HGRN_HEADS + hd
        b_sc, k_sc, c_sc = b_all.at[n], k_all.at[n], c_all.at[n]
        x = src(1)
        lb = lb_ref[:, cs]
        e = jnp.exp(-jnp.abs(x))
        log_sig = jnp.minimum(x, 0.0) - jnp.log(1.0 + e)
        t0 = jnp.log(jnp.maximum(lb, LB_FLOOR))
        t1 = jnp.log1p(-lb) + log_sig
        log_f = jnp.maximum(t0, t1) + jnp.log(1.0 + jnp.exp(-jnp.abs(t0 - t1)))
        kf = (1.0 - lb) * (jnp.where(x > 0.0, e, 1.0) / (1.0 + e))
        qx = src(0)
        qf = qx * _sigmoid(qx)
        v = src(2)
        vb = v.astype(BF16)

        b = log_f * math.log2(math.e)
        shift = 1
        while shift < L:
            b = b + jnp.where(rowv >= shift, pltpu.roll(b, shift, 0), 0.0)
            shift *= 2
        b_sc[...] = b
        lk = jnp.log2(kf)
        k_sc[...] = lk
        c_sc[...] = lk - b

        scores = jnp.zeros((L, L), F32)
        m = C
        while m < L:
            ref_rows = [_row_bcast(b_sc, s0 + m - 1, 2 * m) for s0 in range(0, L, 2 * m)]
            ref = ref_rows[0] if len(ref_rows) == 1 else jnp.concatenate(ref_rows, axis=0)
            upper = ((rowv >> int(math.log2(m))) & 1) == 1
            w = jnp.exp2(jnp.where(upper, b - ref, ref - b))
            qd = jnp.where(upper, qf * w, 0.0).astype(BF16)
            kd = jnp.where(upper, 0.0, kf * w).astype(BF16)
            scores = scores + jnp.where(same_block(2 * m), _dot_nt(qd, kd), 0.0)
            m *= 2

        xs = []
        for s in range(C):
            cs_ = jnp.concatenate([_row_bcast(c_sc, blk0 + s, C) for blk0 in range(0, L, C)], axis=0)
            ks = jnp.concatenate([_row_bcast(k_sc, blk0 + s, C) for blk0 in range(0, L, C)], axis=0)
            xs.append((qf * jnp.exp2(jnp.minimum(b + cs_, ks))).astype(BF16))
        diag = _dot(jnp.concatenate(xs, axis=1), ed_ref[...])
        scores = scores + jnp.where(diag_mask, diag, 0.0)

        st = st_ref[hd]
        o = _dot(scores.astype(BF16), vb) + _dot_nt((qf * jnp.exp2(b)).astype(BF16), st.astype(BF16))

        b_last = _row_bcast(b_sc, L - 1, L)
        kd_end = (kf * jnp.exp2(b_last - b)).astype(BF16)
        st_ref[hd] = st * jnp.exp2(b_last) + _dot(v.T.astype(BF16), kd_end)

        ms = jnp.mean(o * o, axis=-1, keepdims=True)
        gx = src(3)
        o_ref[rs, cs] = o * lax.rsqrt(ms + EPS) * ng_ref[...] * (gx * _sigmoid(gx))
        between(n, n_chunks * HGRN_HEADS)


def _diag_sum_matrix():
    c, d = HGRN_SUB, HGRN_DIM
    rows_s = np.arange(c * d) // d
    cols = np.arange(HGRN_CHUNK) % c
    return jnp.asarray((rows_s[:, None] == cols[None, :]).astype(np.float32), dtype=BF16)


def _proj_hgrn_kernel(x_ref, g_ref, w_ref, cos_ref, sup_ref, sdn_ref, lb_ref, ng_ref, ed_ref, o_ref, ya_ref,
                      st_ref, h_sc, b_all, k_all, c_all):
    @pl.when(pl.program_id(1) == 0)
    def _():
        st_ref[...] = jnp.zeros_like(st_ref)

    x = x_ref[...]
    ms = jnp.mean(x * x, axis=-1, keepdims=True)
    hn = (x * lax.rsqrt(ms + EPS) * g_ref[...]).astype(BF16)
    for s0 in range(0, HGRN_COLS, IN_PROJ_SLAB):
        h_sc[:, s0:s0 + IN_PROJ_SLAB] = _dot(hn, w_ref[:, s0:s0 + IN_PROJ_SLAB])

    rope_scale = {c0: sc for a, w, sc in _ROPE_COLS for c0 in range(a, a + w, LANES)}
    slabs = list(range(0, D_PROJ, IN_PROJ_SLAB))

    def project_slab(s0):
        y = _dot(hn, w_ref[:, HGRN_COLS + s0:HGRN_COLS + s0 + IN_PROJ_SLAB])
        for c0 in range(s0, s0 + IN_PROJ_SLAB, LANES):
            r = y[:, c0 - s0:c0 - s0 + LANES]
            if c0 in rope_scale:
                r = _rope(r, cos_ref[...], sup_ref[...], sdn_ref[...])
                r = r if rope_scale[c0] == 1.0 else r * rope_scale[c0]
            o_ref[:, c0:c0 + LANES] = r

    def between(n, total):
        for k in range(n * len(slabs) // total, (n + 1) * len(slabs) // total):
            project_slab(slabs[k])

    _hgrn_chunks(h_sc, lb_ref, ng_ref, ed_ref, ya_ref, st_ref, b_all, k_all, c_all, IN_PROJ_TM // HGRN_CHUNK, between)


def _proj_hgrn(h, g, w_bf16, tables, lower_bound, norm_g):
    bsz, seq, _ = h.shape
    tm, L = IN_PROJ_TM, HGRN_CHUNK
    chains = (tm // L) * HGRN_HEADS
    per_chain = pltpu.VMEM((chains, L, HGRN_DIM), F32)
    tile = lambda w: pl.BlockSpec((None, tm, w), lambda b, j: (b, j, 0))
    return pl.pallas_call(
        _proj_hgrn_kernel,
        out_shape=(jax.ShapeDtypeStruct((bsz, seq, D_PROJ), F32), jax.ShapeDtypeStruct((bsz, seq, HGRN_WIDTH), F32)),
        grid=(bsz, seq // tm),
        in_specs=[tile(D_MODEL), _resident((1, D_MODEL)), _resident((D_MODEL, D_IN))]
        + [pl.BlockSpec((tm, LANES), lambda b, j: (j, 0))] * 3
        + [_resident((1, HGRN_WIDTH)), _resident((1, HGRN_DIM)), _resident((HGRN_SUB * HGRN_DIM, L))],
        out_specs=(tile(D_PROJ), tile(HGRN_WIDTH)),
        scratch_shapes=[pltpu.VMEM((HGRN_HEADS, HGRN_DIM, HGRN_DIM), F32), pltpu.VMEM((tm, HGRN_COLS), F32),
                        per_chain, per_chain, per_chain],
        compiler_params=_cparams(("parallel", "arbitrary")),
        name="proj_hgrn",
    )(h, g.reshape(1, D_MODEL), w_bf16, *tables, lower_bound.reshape(1, HGRN_WIDTH), norm_g.reshape(1, HGRN_DIM),
      _diag_sum_matrix())


def _s5_tile(u_refs, wb_ref, wc_ref, ar_ref, ai_ref, alr_ref, ali_ref, d_ref, gw_ref, gb_ref, xs, carry, perm_sc,
             between):
    tm = S5_TILE
    lc = tm // S5_LANES
    half = S5_WIDTH

    for c in range(S5_COLS):
        for tau in range(lc):
            perm_sc[c, tau * S5_LANES:(tau + 1) * S5_LANES, :] = u_refs[c][pl.ds(tau, S5_LANES, stride=lc), :]
        xs[:, c * 2 * half:(c + 1) * 2 * half] = _dot(perm_sc[c].astype(BF16), wb_ref[c])

    def cmul_add(ar, ai, xr, xi, br, bi):
        return ar * xr - ai * xi + br, ar * xi + ai * xr + bi

    for c in range(S5_COLS):
        re = slice(c * 2 * half, c * 2 * half + half)
        im = slice(c * 2 * half + half, (c + 1) * 2 * half)
        ar = jnp.broadcast_to(ar_ref[:, c * half:(c + 1) * half], (S5_LANES, half))
        ai = jnp.broadcast_to(ai_ref[:, c * half:(c + 1) * half], (S5_LANES, half))

        sr = si = jnp.zeros((S5_LANES, half), F32)
        for tau in range(lc):
            rows = slice(tau * S5_LANES, (tau + 1) * S5_LANES)
            sr, si = cmul_add(ar, ai, sr, si, xs[rows, re], xs[rows, im])
            xs[rows, re] = sr
            xs[rows, im] = si
        between(c)

    last = (lc - 1) * S5_LANES
    sub_id = lax.broadcasted_iota(jnp.int32, (S5_LANES, half), 0)
    for c in range(S5_COLS):
        re = slice(c * 2 * half, c * 2 * half + half)
        im = slice(c * 2 * half + half, (c + 1) * 2 * half)
        alr, ali = alr_ref[:, c * half:(c + 1) * half], ali_ref[:, c * half:(c + 1) * half]
        gr, gi = carry[0:1, re], carry[0:1, im]
        g_re = jnp.zeros((S5_LANES, half), F32)
        g_im = jnp.zeros((S5_LANES, half), F32)
        for sub in range(S5_LANES):
            g_re = jnp.where(sub_id == sub, gr, g_re)
            g_im = jnp.where(sub_id == sub, gi, g_im)
            gr, gi = cmul_add(alr, ali, gr, gi, xs[pl.ds(last + sub, 1), re], xs[pl.ds(last + sub, 1), im])
        carry[0:1, re] = gr
        carry[0:1, im] = gi
        ar = jnp.broadcast_to(ar_ref[:, c * half:(c + 1) * half], (S5_LANES, half))
        ai = jnp.broadcast_to(ai_ref[:, c * half:(c + 1) * half], (S5_LANES, half))

        cr, ci = g_re, g_im
        for tau in range(lc):
            rows = slice(tau * S5_LANES, (tau + 1) * S5_LANES)
            cr, ci = cmul_add(ar, ai, cr, ci, 0.0, 0.0)
            xs[rows, re] = xs[rows, re] + cr
            xs[rows, im] = xs[rows, im] + ci
        between(S5_COLS + c)

    ys = []
    for c in range(S5_COLS):
        perm_sc[c] = _dot(xs[:, c * 2 * half:(c + 1) * 2 * half].astype(BF16), wc_ref[c])
        y = jnp.concatenate([perm_sc[c, pl.ds(sub, lc, stride=S5_LANES), :] for sub in range(S5_LANES)], axis=0)
        ys.append(y + d_ref[:, c * LANES:(c + 1) * LANES] * u_refs[c][...])
    y = jnp.concatenate(ys, axis=1)
    y = y * (0.5 * (1.0 + jnp.tanh(math.sqrt(2.0 / math.pi) * (y + 0.044715 * (y * y * y)))))
    return y * _sigmoid(_dot(y.astype(BF16), gw_ref[...]) + gb_ref[...])


def _s5_params(a_re, a_im, b_re, b_im, c_re, c_im, log_step):
    a = lax.complex(a_re.astype(F32), a_im.astype(F32))
    dt = jnp.exp(log_step.astype(F32))[:, None]
    a_bar = jnp.exp(dt * a)
    b_bar = ((a_bar - 1.0) / a)[..., None] * lax.complex(b_re.astype(F32), b_im.astype(F32))
    a_l = a_bar ** (S5_TILE // S5_LANES)
    gpc = S5_GROUPS // S5_COLS
    eye = jnp.eye(gpc, dtype=F32)

    def in_w(t):
        t = t.reshape(S5_COLS, gpc, S5_STATE, S5_GROUP)
        return jnp.einsum('cgpi,gh->cgihp', t, eye).reshape(S5_COLS, gpc * S5_GROUP, gpc * S5_STATE)

    def out_w(t):
        t = t.reshape(S5_COLS, gpc, S5_GROUP, S5_STATE)
        return jnp.einsum('cgip,gh->cgphi', t, eye).reshape(S5_COLS, gpc * S5_STATE, gpc * S5_GROUP)

    wb = jnp.concatenate([in_w(jnp.real(b_bar)), in_w(jnp.imag(b_bar))], axis=2).astype(BF16)
    wc = jnp.concatenate([out_w(c_re.astype(F32)), -out_w(c_im.astype(F32))], axis=1).astype(BF16)
    flat = lambda t: t.reshape(1, S5_GROUPS * S5_STATE)
    return wb, wc, flat(jnp.real(a_bar)), flat(jnp.imag(a_bar)), flat(jnp.real(a_l)), flat(jnp.imag(a_l))


def _s5_merge_kernel(h_ref, g_ref, u0_ref, u1_ref, u2_ref, u3_ref, ya_ref, yb_ref,
                     o1_ref, o2_ref, o3_ref, l1_ref, l2_ref, l3_ref,
                     sb_ref, sc_ref, ar_ref, ai_ref, alr_ref, ali_ref, d_ref, gw_ref, gb_ref,
                     wg_ref, wa_ref, wb_ref, wc_ref, wd_ref, wo_ref, out_ref,
                     xs, carry, perm_sc, hn_sc, acc_sc):
    @pl.when(pl.program_id(1) == 0)
    def _():
        carry[...] = jnp.zeros_like(carry)

    h = h_ref[...]
    ms = jnp.mean(h * h, axis=-1, keepdims=True)
    hn_sc[...] = (h * lax.rsqrt(ms + EPS) * g_ref[...]).astype(BF16)
    l1, l2, l3 = l1_ref[...], l2_ref[...], l3_ref[...]
    mx = jnp.maximum(jnp.maximum(l1, l2), l3)
    e1, e2, e3 = jnp.exp(l1 - mx), jnp.exp(l2 - mx), jnp.exp(l3 - mx)
    tot = e1 + e2 + e3
    yd = ((e1 / tot) * o1_ref[...] + (e2 / tot) * o2_ref[...] + (e3 / tot) * o3_ref[...]).astype(BF16)

    half_w = D_MODEL // 2

    def gate(branch, cols):
        return _sigmoid(_dot(hn_sc[...], wg_ref[:, branch * D_MODEL + cols.start:branch * D_MODEL + cols.stop]))

    def unit(k):
        which, cols = k // 2, slice((k % 2) * half_w, (k % 2 + 1) * half_w)
        if which == 0:
            acc_sc[:, cols] = gate(0, cols) * _dot(ya_ref[...].astype(BF16), wa_ref[:, cols])
        elif which == 1:
            acc_sc[:, cols] += gate(1, cols) * _dot(yb_ref[...].astype(BF16), wb_ref[:, cols])
        elif which == 2:
            acc_sc[:, cols] += gate(3, cols) * _dot(yd, wd_ref[:, cols])
        else:
            out_ref[:, cols] = gate(2, cols)

    yc = _s5_tile((u0_ref, u1_ref, u2_ref, u3_ref), sb_ref, sc_ref, ar_ref, ai_ref, alr_ref, ali_ref, d_ref, gw_ref,
                  gb_ref, xs, carry, perm_sc, unit)
    merged = acc_sc[...] + out_ref[...] * _dot(yc.astype(BF16), wc_ref[...])
    out_ref[...] = h + _dot(merged.astype(BF16), wo_ref[...])


def _s5_merge(h, g, proj, s5_params, s5_d, glu_w, glu_b, ya, yb, dil, wg, wa, wb, wc, wd, wo):
    bsz, seq, _ = h.shape
    tm = S5_TILE
    sb, sc, ar, ai, alr, ali = s5_params
    nstate = S5_GROUPS * S5_STATE
    base = OFF_S5 // LANES
    tile = lambda w: pl.BlockSpec((None, tm, w), lambda b, j: (b, j, 0))
    in_specs = [tile(D_MODEL), _resident((1, D_MODEL))]
    in_specs += [pl.BlockSpec((None, tm, LANES), (lambda c: (lambda b, j: (b, j, base + c)))(c)) for c in range(S5_COLS)]
    in_specs += [tile(HGRN_WIDTH), tile(SWA_Q_WIDTH)] + [tile(DIL_OUT)] * 6
    in_specs += [_resident((S5_COLS, LANES, 2 * S5_WIDTH)), _resident((S5_COLS, 2 * S5_WIDTH, LANES))]
    in_specs += [_resident((1, nstate))] * 4
    in_specs += [_resident((1, S5_WIDTH)), _resident((S5_WIDTH, S5_WIDTH)), _resident((1, S5_WIDTH))]
    in_specs += [_resident(w.shape) for w in (wg, wa, wb, wc, wd, wo)]
    (o1, s1), (o2, s2), (o3, s3) = dil
    return pl.pallas_call(
        _s5_merge_kernel,
        out_shape=jax.ShapeDtypeStruct((bsz, seq, D_MODEL), F32),
        grid=(bsz, seq // tm),
        in_specs=in_specs,
        out_specs=tile(D_MODEL),
        scratch_shapes=[pltpu.VMEM((tm, 2 * nstate), F32), pltpu.VMEM((S5_LANES, 2 * nstate), F32),
                        pltpu.VMEM((S5_COLS, tm, LANES), F32), pltpu.VMEM((tm, D_MODEL), BF16),
                        pltpu.VMEM((tm, D_MODEL), F32)],
        compiler_params=_cparams(("parallel", "arbitrary"), VMEM_LIMIT_S5_MERGE),
        name="s5_merge",
    )(h, g.reshape(1, D_MODEL), proj, proj, proj, proj, ya, yb, o1, o2, o3, s1, s2, s3, sb, sc, ar, ai, alr, ali,
      s5_d.reshape(1, S5_WIDTH), glu_w.astype(BF16), glu_b.reshape(1, S5_WIDTH), wg, wa, wb, wc, wd, wo)


def _ffn_kernel(h_ref, hp_ref, g_ref, wu_ref, cw_ref, cb_ref, wd_ref, fg_ref, o_ref, hn_sc, z_sc, *, final_norm):
    i = pl.program_id(1)
    tm, tf, halo = FFN_TM, FFN_TF, FFN_HALO

    def norm(x):
        ms = jnp.mean(x * x, axis=-1, keepdims=True)
        return x * lax.rsqrt(ms + EPS) * g_ref[...]

    def stage(t):
        hn_sc[t, halo:, :] = norm(h_ref[t * tm:(t + 1) * tm, :]).astype(BF16)
        if t == 0:
            hn_sc[t, :halo, :] = jnp.where(i == 0, 0.0, norm(hp_ref[...])).astype(BF16)
        else:
            hn_sc[t, :halo, :] = norm(h_ref[t * tm - halo:t * tm, :]).astype(BF16)

    def conv(t, cols):
        u = _dot(hn_sc[t], wu_ref[:, cols])
        u1 = pltpu.roll(u, 1, 0)
        u2 = pltpu.roll(u, 2, 0)
        return (cw_ref[0:1, cols] * u2[halo:, :] + cw_ref[1:2, cols] * u1[halo:, :] + cw_ref[2:3, cols] * u[halo:, :]
                + cb_ref[:, cols])

    stage(0)
    for t in range(FFN_SUBTILES):
        for c in range(D_FF // tf):
            a = conv(t, slice(c * tf, (c + 1) * tf))
            b = conv(t, slice(D_FF + c * tf, D_FF + (c + 1) * tf))
            z_sc[t, :, c * tf:(c + 1) * tf] = ((a * _sigmoid(a)) * b).astype(BF16)
            if c == 0 and t + 1 < FFN_SUBTILES:
                stage(t + 1)
        rows = slice(t * tm, (t + 1) * tm)
        y = h_ref[rows, :] + _dot(z_sc[t], wd_ref[...])
        if final_norm:
            ms = jnp.mean(y * y, axis=-1, keepdims=True)
            y = y * lax.rsqrt(ms + EPS) * fg_ref[...]
        o_ref[rows, :] = y


def _ffn(h, g, w_up, conv_w, conv_b, w_down, final_g, final_norm):
    bsz, seq, _ = h.shape
    tm, halo = FFN_TM, FFN_HALO
    rows = FFN_SUBTILES * tm
    return pl.pallas_call(
        functools.partial(_ffn_kernel, final_norm=final_norm),
        out_shape=jax.ShapeDtypeStruct((bsz, seq, D_MODEL), F32),
        grid=(bsz, seq // rows),
        in_specs=[pl.BlockSpec((None, rows, D_MODEL), lambda b, i: (b, i, 0)),
                  pl.BlockSpec((None, halo, D_MODEL), lambda b, i: (b, jnp.maximum(i * (rows // halo) - 1, 0), 0)),
                  _resident((1, D_MODEL)), _resident((D_MODEL, 2 * D_FF)), _resident((CONV_WIDTH, 2 * D_FF)),
                  _resident((1, 2 * D_FF)), _resident((D_FF, D_MODEL)), _resident((1, D_MODEL))],
        out_specs=pl.BlockSpec((None, rows, D_MODEL), lambda b, i: (b, i, 0)),
        scratch_shapes=[pltpu.VMEM((FFN_SUBTILES, tm + halo, D_MODEL), BF16),
                        pltpu.VMEM((FFN_SUBTILES, tm, D_FF), BF16)],
        compiler_params=_cparams(("parallel", "arbitrary")),
        name="conv_ffn",
    )(h, h, g.reshape(1, D_MODEL), w_up, conv_w, conv_b.reshape(1, -1), w_down, final_g.reshape(1, D_MODEL))


def kernel(x, norm1_g, w_in, hgrn_lb_logits, hgrn_norm_g, attn_sinks, s5_a_re, s5_a_im, s5_b_re, s5_b_im, s5_c_re, s5_c_im, s5_d, s5_log_step, s5_glu_w, s5_glu_b, w_branch_a, w_branch_b, w_branch_c, w_branch_d, w_out, norm2_g, ffn_w_up, ffn_conv_w, ffn_conv_b, ffn_w_down, final_norm_g):
    bsz, seq, _ = x.shape
    tables = _rope_tables(seq)
    p = jax.nn.softmax(hgrn_lb_logits.astype(F32), axis=0)
    lower_bounds = jnp.cumsum(p, axis=0) - p[0]

    w_in_bf16 = w_in.astype(BF16)

    h = x
    for l in range(DEPTH):
        proj, y_a = _proj_hgrn(h, norm1_g[l], w_in_bf16[l], tables, lower_bounds[l], hgrn_norm_g[l])
        y_b = _band_attn(proj, OFF_SWA_Q, OFF_SWA_K, OFF_SWA_V, SWA_Q_WIDTH, SWA_KV_WIDTH, dil=1, nq=SWA_NQ,
                         max_dist=SWA_WINDOW - 1, sinks=attn_sinks[l].astype(F32))
        dil = [_band_attn(proj, OFF_DIL_Q + g * DIL_OUT, OFF_DIL_K + g * DIL_OUT, OFF_DIL_V + g * DIL_OUT, DIL_OUT,
                          DIL_OUT, dil=dilation, nq=DIL_NQ[g], max_dist=window // dilation)
               for g, (window, dilation) in enumerate(DIL_PAIRS)]
        s5_params = _s5_params(s5_a_re[l], s5_a_im[l], s5_b_re[l], s5_b_im[l], s5_c_re[l], s5_c_im[l], s5_log_step[l])
        h = _s5_merge(h, norm1_g[l], proj, s5_params, s5_d[l], s5_glu_w[l], s5_glu_b[l], y_a, y_b, dil,
                      w_in_bf16[l][:, D_IN:], w_branch_a[l].astype(BF16), w_branch_b[l].astype(BF16),
                      w_branch_c[l].astype(BF16), w_branch_d[l].astype(BF16), w_out[l].astype(BF16))
        h = _ffn(h, norm2_g[l], ffn_w_up[l].astype(BF16), ffn_conv_w[l], ffn_conv_b[l], ffn_w_down[l].astype(BF16),
                 final_norm_g, final_norm=(l == DEPTH - 1))
    return h
```

```python
import functools
import math

import numpy as np
import jax
import jax.numpy as jnp
from jax import lax
from jax.experimental import pallas as pl
from jax.experimental.pallas import tpu as pltpu

F32 = jnp.float32
BF16 = jnp.bfloat16

D_MODEL = 1024
DEPTH = 2
HEAD_DIM = 64
BLOCK = 128
ROPE_THETA = 500000.0
ROT_DIM = HEAD_DIM // 4
EPS = 1e-6
MASK_VALUE = -1e30
LB_FLOOR = 1e-30
N_BRANCH = 4

HGRN_HEADS = 4
HGRN_DIM = 128
HGRN_WIDTH = HGRN_HEADS * HGRN_DIM
HGRN_CHUNK = 128
HGRN_SUB = 8

SWA_Q_HEADS = 8
SWA_KV_HEADS = 2
SWA_WINDOW = 128
SWA_Q_WIDTH = SWA_Q_HEADS * HEAD_DIM
SWA_KV_WIDTH = SWA_KV_HEADS * HEAD_DIM

S5_WIDTH = 512
S5_GROUP = 16
S5_GROUPS = S5_WIDTH // S5_GROUP
S5_STATE = 64
S5_COLS = 4
S5_TILE = 512
S5_LANES = 8

DIL_PAIRS = ((128, 1), (512, 4), (2048, 16))
DIL_HEADS_PER_GROUP = 4
DIL_HEADS = DIL_HEADS_PER_GROUP * len(DIL_PAIRS)
DIL_OUT = DIL_HEADS_PER_GROUP * HEAD_DIM
DIL_NQ = (16, 4, 1)
SWA_NQ = 16

D_FF = 2816
CONV_WIDTH = 3
FFN_TF = 256
FFN_TM = 512
FFN_SUBTILES = 2
FFN_HALO = 16

LANES = 128

HGRN_COLS = 4 * HGRN_WIDTH
OFF_SWA_Q = 0
OFF_SWA_K = OFF_SWA_Q + SWA_Q_WIDTH
OFF_SWA_V = OFF_SWA_K + SWA_KV_WIDTH
OFF_S5 = OFF_SWA_V + SWA_KV_WIDTH
OFF_DIL_Q = OFF_S5 + S5_WIDTH
OFF_DIL_K = OFF_DIL_Q + DIL_HEADS * HEAD_DIM
OFF_DIL_V = OFF_DIL_K + DIL_HEADS * HEAD_DIM
D_PROJ = OFF_DIL_V + DIL_HEADS * HEAD_DIM
D_IN = HGRN_COLS + D_PROJ

VMEM_LIMIT = 56 * 1024 * 1024
VMEM_LIMIT_S5_MERGE = 60 * 1024 * 1024


def _cparams(sem, vmem_limit=VMEM_LIMIT):
    return pltpu.CompilerParams(dimension_semantics=sem, vmem_limit_bytes=vmem_limit)


def _resident(shape):
    return pl.BlockSpec(shape, lambda *_: (0,) * len(shape), pipeline_mode=pl.Buffered(1))


def _sigmoid(x):
    return 0.5 * jnp.tanh(0.5 * x) + 0.5


def _row_bcast(ref, r, n):
    return jnp.broadcast_to(ref[pl.ds(r, 1), :], (n, ref.shape[1]))


def _dot(a, b):
    return jnp.dot(a, b, preferred_element_type=F32)


def _dot_nt(a, b):
    return lax.dot_general(a, b, (((1,), (1,)), ((), ())), preferred_element_type=F32)


_ROPE_COLS = ((OFF_SWA_Q, SWA_Q_WIDTH, HEAD_DIM ** -0.5), (OFF_DIL_Q, DIL_HEADS * HEAD_DIM, HEAD_DIM ** -0.5),
              (OFF_DIL_K, DIL_HEADS * HEAD_DIM, 1.0), (OFF_SWA_K, SWA_KV_WIDTH, 1.0))


IN_PROJ_SLAB = 256
IN_PROJ_TM = 512


def _rope_tables(seq):
    half = ROT_DIM // 2
    inv_freq = ROPE_THETA ** (-jnp.arange(0, ROT_DIM, 2, dtype=F32) / ROT_DIM)
    dim = np.arange(LANES) % HEAD_DIM
    ang = jnp.arange(seq, dtype=F32)[:, None] * inv_freq[dim % half][None, :]
    cos, sin = jnp.cos(ang), jnp.sin(ang)
    cos_t = jnp.where(dim < ROT_DIM, cos, 1.0)
    sup_t = jnp.where(dim < half, -sin, 0.0)
    sdn_t = jnp.where((dim >= half) & (dim < ROT_DIM), sin, 0.0)
    return cos_t, sup_t, sdn_t


def _rope(x, cos, sup, sdn):
    half = ROT_DIM // 2
    return x * cos + pltpu.roll(x, LANES - half, 1) * sup + pltpu.roll(x, half, 1) * sdn


def _band_attn_kernel(*refs, dil, nq, max_dist, has_sink, gqa, pairs):
    if has_sink:
        sink_ref, refs = refs[0], refs[1:]
    q_ref, kc_ref, kp_ref, vc_ref, vp_ref, o_ref = refs[:6]
    refs = refs[6:]
    lse_ref = None
    if not has_sink:
        lse_ref, refs = refs[0], refs[1:]
    j = pl.program_id(1)
    pair0 = pl.program_id(2) * pairs
    blk = BLOCK * dil
    span = nq * blk

    lane = lax.broadcasted_iota(jnp.int32, (1, LANES), 1)
    lo_half = lane < HEAD_DIM

    if gqa:
        k_sc, v_sc = refs
        for sc, prev_ref, cur_ref in ((k_sc, kp_ref, kc_ref), (v_sc, vp_ref, vc_ref)):
            for src, base, n in ((prev_ref, 0, blk), (cur_ref, blk, span)):
                for r0 in range(0, n, min(n, 2 * BLOCK)):
                    rs = slice(r0, r0 + min(n, 2 * BLOCK))
                    t = src[rs, :]
                    sw = pltpu.roll(t, HEAD_DIM, 1)
                    sc[0, base + r0:base + rs.stop, :] = jnp.where(lo_half, t, sw)
                    sc[1, base + r0:base + rs.stop, :] = jnp.where(lo_half, sw, t)

    per_kv = SWA_Q_HEADS // (2 * SWA_KV_HEADS) if gqa else 1
    stack = 2 * per_kv
    qi = lax.broadcasted_iota(jnp.int32, (stack * BLOCK, 2 * BLOCK), 0) & (BLOCK - 1)
    ki = lax.broadcasted_iota(jnp.int32, (stack * BLOCK, 2 * BLOCK), 1)
    dist = qi + BLOCK - ki
    band = (dist >= 0) & (dist <= max_dist)
    head_of_row = lax.broadcasted_iota(jnp.int32, (stack * BLOCK, 1), 0) >> int(math.log2(BLOCK))

    def rows(start, size):
        return pl.ds(start, size) if dil == 1 else pl.ds(start, size, stride=dil)

    def load_kv(cur_ref, prev_ref, sc, i, res, p):
        if gqa:
            return sc[p // (SWA_Q_HEADS // (2 * SWA_KV_HEADS)), i * blk:i * blk + 2 * BLOCK, :]
        cols = slice(p * LANES, (p + 1) * LANES)
        prev = prev_ref[rows(res, BLOCK), cols] if i == 0 else cur_ref[rows((i - 1) * blk + res, BLOCK), cols]
        return jnp.concatenate([prev, cur_ref[rows(i * blk + res, BLOCK), cols]], axis=0)

    for res in range(dil):
        for i in range(nq):
            valid = band & (ki >= jnp.where(j == 0, BLOCK, 0)) if i == 0 else band
            q_rows = rows(i * blk + res, BLOCK)
            for p0 in range(0, pairs, per_kv):
                kp = load_kv(kc_ref, kp_ref, k_sc if gqa else None, i, res, p0).astype(BF16)
                vp = load_kv(vc_ref, vp_ref, v_sc if gqa else None, i, res, p0).astype(BF16)
                qs = []
                for p in range(p0, p0 + per_kv):
                    qp = q_ref[q_rows, p * LANES:(p + 1) * LANES]
                    qs += [jnp.where(lo_half, qp, 0.0), jnp.where(lo_half, 0.0, qp)]
                s = jnp.where(valid, _dot_nt(jnp.concatenate(qs, axis=0).astype(BF16), kp), MASK_VALUE)
                m = jnp.max(s, axis=-1, keepdims=True)
                if has_sink:
                    sk = sink_ref[2 * (pair0 + p0)]
                    for hh in range(1, stack):
                        sk = jnp.where(head_of_row == hh, sink_ref[2 * (pair0 + p0) + hh], sk)
                    m = jnp.maximum(m, sk)
                pe = jnp.exp(s - m)
                den = jnp.sum(pe, axis=-1, keepdims=True)
                if has_sink:
                    den = den + jnp.exp(sk - m)
                o = _dot(pe.astype(BF16), vp) / den
                lse = m + jnp.log(den)
                for p in range(p0, p0 + per_kv):
                    r0 = 2 * (p - p0) * BLOCK
                    cols = slice(p * LANES, (p + 1) * LANES)
                    o_ref[q_rows, cols] = jnp.where(lo_half, o[r0:r0 + BLOCK], o[r0 + BLOCK:r0 + 2 * BLOCK])
                    if lse_ref is not None:
                        lse_ref[q_rows, cols] = jnp.where(lo_half, lse[r0:r0 + BLOCK], lse[r0 + BLOCK:r0 + 2 * BLOCK])


def _band_attn(proj, q_off, k_off, v_off, width, kv_width, dil, nq, max_dist, sinks=None):
    bsz, seq, _ = proj.shape
    blk = BLOCK * dil
    span = nq * blk
    gqa = kv_width != width
    n_pairs = width // LANES
    pairs = n_pairs if dil == 1 else 1
    assert not gqa or (dil == 1 and kv_width == LANES)
    qw = pairs * LANES
    kw = kv_width if gqa else qw
    cur = lambda off, w: (lambda b, j, p: (b, j, off // w + p))
    prev = lambda off, w: (lambda b, j, p: (b, jnp.maximum(j * nq - 1, 0), off // w + p))
    in_specs = [pl.BlockSpec((None, span, qw), cur(q_off, qw)),
                pl.BlockSpec((None, span, kw), cur(k_off, kw)), pl.BlockSpec((None, blk, kw), prev(k_off, kw)),
                pl.BlockSpec((None, span, kw), cur(v_off, kw)), pl.BlockSpec((None, blk, kw), prev(v_off, kw))]
    args = [proj] * 5
    has_sink = sinks is not None
    if has_sink:
        in_specs = [pl.BlockSpec(memory_space=pltpu.SMEM)] + in_specs
        args = [sinks] + args
    o_spec = pl.BlockSpec((None, span, qw), lambda b, j, p: (b, j, p))
    o_shape = jax.ShapeDtypeStruct((bsz, seq, width), F32)
    out_shape, out_specs = (o_shape, o_spec) if has_sink else ((o_shape, o_shape), (o_spec, o_spec))
    return pl.pallas_call(
        functools.partial(_band_attn_kernel, dil=dil, nq=nq, max_dist=max_dist, has_sink=has_sink, gqa=gqa, pairs=pairs),
        out_shape=out_shape,
        grid=(bsz, seq // span, n_pairs // pairs),
        in_specs=in_specs,
        out_specs=out_specs,
        scratch_shapes=[pltpu.VMEM((SWA_KV_HEADS, span + blk, LANES), F32)] * 2 if gqa else [],
        compiler_params=_cparams(("parallel", "arbitrary", "arbitrary")),
        name="swa_attn" if has_sink else f"dil_attn_{dil}",
    )(*args)


def _hgrn_chunks(src_ref, lb_ref, ng_ref, ed_ref, o_ref, st_ref, b_all, k_all, c_all, n_chunks, between):
    L, C, D = HGRN_CHUNK, HGRN_SUB, HGRN_DIM

    row = lax.broadcasted_iota(jnp.int32, (L, L), 0)
    col = lax.broadcasted_iota(jnp.int32, (L, L), 1)
    rowv = lax.broadcasted_iota(jnp.int32, (L, D), 0)
    same_block = lambda size: (row >> int(math.log2(size))) == (col >> int(math.log2(size)))
    diag_mask = same_block(C) & (col <= row)

    for ck, hd in [(ck, hd) for ck in range(n_chunks) for hd in range(HGRN_HEADS)]:
        cs = slice(hd * D, (hd + 1) * D)
        rs = slice(ck * L, (ck + 1) * L)
        src = lambda kind, rs=rs, hd=hd: src_ref[rs, kind * HGRN_WIDTH + hd * D:kind * HGRN_WIDTH + (hd + 1) * D]
        n = ck * HGRN_HEADS + hd
        b_sc, k_sc, c_sc = b_all.at[n], k_all.at[n], c_all.at[n]
        x = src(1)
        lb = lb_ref[:, cs]
        e = jnp.exp(-jnp.abs(x))
        log_sig = jnp.minimum(x, 0.0) - jnp.log(1.0 + e)
        t0 = jnp.log(jnp.maximum(lb, LB_FLOOR))
        t1 = jnp.log1p(-lb) + log_sig
        log_f = jnp.maximum(t0, t1) + jnp.log(1.0 + jnp.exp(-jnp.abs(t0 - t1)))
        kf = (1.0 - lb) * (jnp.where(x > 0.0, e, 1.0) / (1.0 + e))
        qx = src(0)
        qf = qx * _sigmoid(qx)
        v = src(2)
        vb = v.astype(BF16)

        b = log_f * math.log2(math.e)
        shift = 1
        while shift < L:
            b = b + jnp.where(rowv >= shift, pltpu.roll(b, shift, 0), 0.0)
            shift *= 2
        b_sc[...] = b
        lk = jnp.log2(kf)
        k_sc[...] = lk
        c_sc[...] = lk - b

        scores = jnp.zeros((L, L), F32)
        m = C
        while m < L:
            ref_rows = [_row_bcast(b_sc, s0 + m - 1, 2 * m) for s0 in range(0, L, 2 * m)]
            ref = ref_rows[0] if len(ref_rows) == 1 else jnp.concatenate(ref_rows, axis=0)
            upper = ((rowv >> int(math.log2(m))) & 1) == 1
            w = jnp.exp2(jnp.where(upper, b - ref, ref - b))
            qd = jnp.where(upper, qf * w, 0.0).astype(BF16)
            kd = jnp.where(upper, 0.0, kf * w).astype(BF16)
            scores = scores + jnp.where(same_block(2 * m), _dot_nt(qd, kd), 0.0)
            m *= 2

        xs = []
        for s in range(C):
            cs_ = jnp.concatenate([_row_bcast(c_sc, blk0 + s, C) for blk0 in range(0, L, C)], axis=0)
            ks = jnp.concatenate([_row_bcast(k_sc, blk0 + s, C) for blk0 in range(0, L, C)], axis=0)
            xs.append((qf * jnp.exp2(jnp.minimum(b + cs_, ks))).astype(BF16))
        diag = _dot(jnp.concatenate(xs, axis=1), ed_ref[...])
        scores = scores + jnp.where(diag_mask, diag, 0.0)

        st = st_ref[hd]
        o = _dot(scores.astype(BF16), vb) + _dot_nt((qf * jnp.exp2(b)).astype(BF16), st.astype(BF16))

        b_last = _row_bcast(b_sc, L - 1, L)
        kd_end = (kf * jnp.exp2(b_last - b)).astype(BF16)
        st_ref[hd] = st * jnp.exp2(b_last) + _dot(v.T.astype(BF16), kd_end)

        ms = jnp.mean(o * o, axis=-1, keepdims=True)
        gx = src(3)
        o_ref[rs, cs] = o * lax.rsqrt(ms + EPS) * ng_ref[...] * (gx * _sigmoid(gx))
        between(n, n_chunks * HGRN_HEADS)


def _diag_sum_matrix():
    c, d = HGRN_SUB, HGRN_DIM
    rows_s = np.arange(c * d) // d
    cols = np.arange(HGRN_CHUNK) % c
    return jnp.asarray((rows_s[:, None] == cols[None, :]).astype(np.float32), dtype=BF16)


def _proj_hgrn_kernel(x_ref, g_ref, w_ref, cos_ref, sup_ref, sdn_ref, lb_ref, ng_ref, ed_ref, o_ref, ya_ref,
                      st_ref, h_sc, b_all, k_all, c_all):
    @pl.when(pl.program_id(1) == 0)
    def _():
        st_ref[...] = jnp.zeros_like(st_ref)

    x = x_ref[...]
    ms = jnp.mean(x * x, axis=-1, keepdims=True)
    hn = (x * lax.rsqrt(ms + EPS) * g_ref[...]).astype(BF16)
    for s0 in range(0, HGRN_COLS, IN_PROJ_SLAB):
        h_sc[:, s0:s0 + IN_PROJ_SLAB] = _dot(hn, w_ref[:, s0:s0 + IN_PROJ_SLAB])

    rope_scale = {c0: sc for a, w, sc in _ROPE_COLS for c0 in range(a, a + w, LANES)}
    slabs = list(range(0, D_PROJ, IN_PROJ_SLAB))

    def project_slab(s0):
        y = _dot(hn, w_ref[:, HGRN_COLS + s0:HGRN_COLS + s0 + IN_PROJ_SLAB])
        for c0 in range(s0, s0 + IN_PROJ_SLAB, LANES):
            r = y[:, c0 - s0:c0 - s0 + LANES]
            if c0 in rope_scale:
                r = _rope(r, cos_ref[...], sup_ref[...], sdn_ref[...])
                r = r if rope_scale[c0] == 1.0 else r * rope_scale[c0]
            o_ref[:, c0:c0 + LANES] = r

    def between(n, total):
        for k in range(n * len(slabs) // total, (n + 1) * len(slabs) // total):
            project_slab(slabs[k])

    _hgrn_chunks(h_sc, lb_ref, ng_ref, ed_ref, ya_ref, st_ref, b_all, k_all, c_all, IN_PROJ_TM // HGRN_CHUNK, between)


def _proj_hgrn(h, g, w_bf16, tables, lower_bound, norm_g):
    bsz, seq, _ = h.shape
    tm, L = IN_PROJ_TM, HGRN_CHUNK
    chains = (tm // L) * HGRN_HEADS
    per_chain = pltpu.VMEM((chains, L, HGRN_DIM), F32)
    tile = lambda w: pl.BlockSpec((None, tm, w), lambda b, j: (b, j, 0))
    return pl.pallas_call(
        _proj_hgrn_kernel,
        out_shape=(jax.ShapeDtypeStruct((bsz, seq, D_PROJ), F32), jax.ShapeDtypeStruct((bsz, seq, HGRN_WIDTH), F32)),
        grid=(bsz, seq // tm),
        in_specs=[tile(D_MODEL), _resident((1, D_MODEL)), _resident((D_MODEL, D_IN))]
        + [pl.BlockSpec((tm, LANES), lambda b, j: (j, 0))] * 3
        + [_resident((1, HGRN_WIDTH)), _resident((1, HGRN_DIM)), _resident((HGRN_SUB * HGRN_DIM, L))],
        out_specs=(tile(D_PROJ), tile(HGRN_WIDTH)),
        scratch_shapes=[pltpu.VMEM((HGRN_HEADS, HGRN_DIM, HGRN_DIM), F32), pltpu.VMEM((tm, HGRN_COLS), F32),
                        per_chain, per_chain, per_chain],
        compiler_params=_cparams(("parallel", "arbitrary")),
        name="proj_hgrn",
    )(h, g.reshape(1, D_MODEL), w_bf16, *tables, lower_bound.reshape(1, HGRN_WIDTH), norm_g.reshape(1, HGRN_DIM),
      _diag_sum_matrix())


def _s5_tile(u_refs, wb_ref, wc_ref, ar_ref, ai_ref, alr_ref, ali_ref, d_ref, gw_ref, gb_ref, xs, carry, perm_sc,
             between):
    tm = S5_TILE
    lc = tm // S5_LANES
    half = S5_WIDTH

    for c in range(S5_COLS):
        for tau in range(lc):
            perm_sc[c, tau * S5_LANES:(tau + 1) * S5_LANES, :] = u_refs[c][pl.ds(tau, S5_LANES, stride=lc), :]
        xs[:, c * 2 * half:(c + 1) * 2 * half] = _dot(perm_sc[c].astype(BF16), wb_ref[c])

    def cmul_add(ar, ai, xr, xi, br, bi):
        return ar * xr - ai * xi + br, ar * xi + ai * xr + bi

    for c in range(S5_COLS):
        re = slice(c * 2 * half, c * 2 * half + half)
        im = slice(c * 2 * half + half, (c + 1) * 2 * half)
        ar = jnp.broadcast_to(ar_ref[:, c * half:(c + 1) * half], (S5_LANES, half))
        ai = jnp.broadcast_to(ai_ref[:, c * half:(c + 1) * half], (S5_LANES, half))

        sr = si = jnp.zeros((S5_LANES, half), F32)
        for tau in range(lc):
            rows = slice(tau * S5_LANES, (tau + 1) * S5_LANES)
            sr, si = cmul_add(ar, ai, sr, si, xs[rows, re], xs[rows, im])
            xs[rows, re] = sr
            xs[rows, im] = si
        between(c)

    last = (lc - 1) * S5_LANES
    sub_id = lax.broadcasted_iota(jnp.int32, (S5_LANES, half), 0)
    for c in range(S5_COLS):
        re = slice(c * 2 * half, c * 2 * half + half)
        im = slice(c * 2 * half + half, (c + 1) * 2 * half)
        alr, ali = alr_ref[:, c * half:(c + 1) * half], ali_ref[:, c * half:(c + 1) * half]
        gr, gi = carry[0:1, re], carry[0:1, im]
        g_re = jnp.zeros((S5_LANES, half), F32)
        g_im = jnp.zeros((S5_LANES, half), F32)
        for sub in range(S5_LANES):
            g_re = jnp.where(sub_id == sub, gr, g_re)
            g_im = jnp.where(sub_id == sub, gi, g_im)
            gr, gi = cmul_add(alr, ali, gr, gi, xs[pl.ds(last + sub, 1), re], xs[pl.ds(last + sub, 1), im])
        carry[0:1, re] = gr
        carry[0:1, im] = gi
        ar = jnp.broadcast_to(ar_ref[:, c * half:(c + 1) * half], (S5_LANES, half))
        ai = jnp.broadcast_to(ai_ref[:, c * half:(c + 1) * half], (S5_LANES, half))

        cr, ci = g_re, g_im
        for tau in range(lc):
            rows = slice(tau * S5_LANES, (tau + 1) * S5_LANES)
            cr, ci = cmul_add(ar, ai, cr, ci, 0.0, 0.0)
            xs[rows, re] = xs[rows, re] + cr
            xs[rows, im] = xs[rows, im] + ci
        between(S5_COLS + c)

    ys = []
    for c in range(S5_COLS):
        perm_sc[c] = _dot(xs[:, c * 2 * half:(c + 1) * 2 * half].astype(BF16), wc_ref[c])
        y = jnp.concatenate([perm_sc[c, pl.ds(sub, lc, stride=S5_LANES), :] for sub in range(S5_LANES)], axis=0)
        ys.append(y + d_ref[:, c * LANES:(c + 1) * LANES] * u_refs[c][...])
    y = jnp.concatenate(ys, axis=1)
    y = y * (0.5 * (1.0 + jnp.tanh(math.sqrt(2.0 / math.pi) * (y + 0.044715 * (y * y * y)))))
    return y * _sigmoid(_dot(y.astype(BF16), gw_ref[...]) + gb_ref[...])


def _s5_params(a_re, a_im, b_re, b_im, c_re, c_im, log_step):
    a = lax.complex(a_re.astype(F32), a_im.astype(F32))
    dt = jnp.exp(log_step.astype(F32))[:, None]
    a_bar = jnp.exp(dt * a)
    b_bar = ((a_bar - 1.0) / a)[..., None] * lax.complex(b_re.astype(F32), b_im.astype(F32))
    a_l = a_bar ** (S5_TILE // S5_LANES)
    gpc = S5_GROUPS // S5_COLS
    eye = jnp.eye(gpc, dtype=F32)

    def in_w(t):
        t = t.reshape(S5_COLS, gpc, S5_STATE, S5_GROUP)
        return jnp.einsum('cgpi,gh->cgihp', t, eye).reshape(S5_COLS, gpc * S5_GROUP, gpc * S5_STATE)

    def out_w(t):
        t = t.reshape(S5_COLS, gpc, S5_GROUP, S5_STATE)
        return jnp.einsum('cgip,gh->cgphi', t, eye).reshape(S5_COLS, gpc * S5_STATE, gpc * S5_GROUP)

    wb = jnp.concatenate([in_w(jnp.real(b_bar)), in_w(jnp.imag(b_bar))], axis=2).astype(BF16)
    wc = jnp.concatenate([out_w(c_re.astype(F32)), -out_w(c_im.astype(F32))], axis=1).astype(BF16)
    flat = lambda t: t.reshape(1, S5_GROUPS * S5_STATE)
    return wb, wc, flat(jnp.real(a_bar)), flat(jnp.imag(a_bar)), flat(jnp.real(a_l)), flat(jnp.imag(a_l))


def _s5_merge_kernel(h_ref, g_ref, u0_ref, u1_ref, u2_ref, u3_ref, ya_ref, yb_ref,
                     o1_ref, o2_ref, o3_ref, l1_ref, l2_ref, l3_ref,
                     sb_ref, sc_ref, ar_ref, ai_ref, alr_ref, ali_ref, d_ref, gw_ref, gb_ref,
                     wg_ref, wa_ref, wb_ref, wc_ref, wd_ref, wo_ref, out_ref,
                     xs, carry, perm_sc, hn_sc, acc_sc):
    @pl.when(pl.program_id(1) == 0)
    def _():
        carry[...] = jnp.zeros_like(carry)

    h = h_ref[...]
    ms = jnp.mean(h * h, axis=-1, keepdims=True)
    hn_sc[...] = (h * lax.rsqrt(ms + EPS) * g_ref[...]).astype(BF16)
    l1, l2, l3 = l1_ref[...], l2_ref[...], l3_ref[...]
    mx = jnp.maximum(jnp.maximum(l1, l2), l3)
    e1, e2, e3 = jnp.exp(l1 - mx), jnp.exp(l2 - mx), jnp.exp(l3 - mx)
    tot = e1 + e2 + e3
    yd = ((e1 / tot) * o1_ref[...] + (e2 / tot) * o2_ref[...] + (e3 / tot) * o3_ref[...]).astype(BF16)

    half_w = D_MODEL // 2

    def gate(branch, cols):
        return _sigmoid(_dot(hn_sc[...], wg_ref[:, branch * D_MODEL + cols.start:branch * D_MODEL + cols.stop]))

    def unit(k):
        which, cols = k // 2, slice((k % 2) * half_w, (k % 2 + 1) * half_w)
        if which == 0:
            acc_sc[:, cols] = gate(0, cols) * _dot(ya_ref[...].astype(BF16), wa_ref[:, cols])
        elif which == 1:
            acc_sc[:, cols] += gate(1, cols) * _dot(yb_ref[...].astype(BF16), wb_ref[:, cols])
        elif which == 2:
            acc_sc[:, cols] += gate(3, cols) * _dot(yd, wd_ref[:, cols])
        else:
            out_ref[:, cols] = gate(2, cols)

    yc = _s5_tile((u0_ref, u1_ref, u2_ref, u3_ref), sb_ref, sc_ref, ar_ref, ai_ref, alr_ref, ali_ref, d_ref, gw_ref,
                  gb_ref, xs, carry, perm_sc, unit)
    merged = acc_sc[...] + out_ref[...] * _dot(yc.astype(BF16), wc_ref[...])
    out_ref[...] = h + _dot(merged.astype(BF16), wo_ref[...])


def _s5_merge(h, g, proj, s5_params, s5_d, glu_w, glu_b, ya, yb, dil, wg, wa, wb, wc, wd, wo):
    bsz, seq, _ = h.shape
    tm = S5_TILE
    sb, sc, ar, ai, alr, ali = s5_params
    nstate = S5_GROUPS * S5_STATE
    base = OFF_S5 // LANES
    tile = lambda w: pl.BlockSpec((None, tm, w), lambda b, j: (b, j, 0))
    in_specs = [tile(D_MODEL), _resident((1, D_MODEL))]
    in_specs += [pl.BlockSpec((None, tm, LANES), (lambda c: (lambda b, j: (b, j, base + c)))(c)) for c in range(S5_COLS)]
    in_specs += [tile(HGRN_WIDTH), tile(SWA_Q_WIDTH)] + [tile(DIL_OUT)] * 6
    in_specs += [_resident((S5_COLS, LANES, 2 * S5_WIDTH)), _resident((S5_COLS, 2 * S5_WIDTH, LANES))]
    in_specs += [_resident((1, nstate))] * 4
    in_specs += [_resident((1, S5_WIDTH)), _resident((S5_WIDTH, S5_WIDTH)), _resident((1, S5_WIDTH))]
    in_specs += [_resident(w.shape) for w in (wg, wa, wb, wc, wd, wo)]
    (o1, s1), (o2, s2), (o3, s3) = dil
    return pl.pallas_call(
        _s5_merge_kernel,
        out_shape=jax.ShapeDtypeStruct((bsz, seq, D_MODEL), F32),
        grid=(bsz, seq // tm),
        in_specs=in_specs,
        out_specs=tile(D_MODEL),
        scratch_shapes=[pltpu.VMEM((tm, 2 * nstate), F32), pltpu.VMEM((S5_LANES, 2 * nstate), F32),
                        pltpu.VMEM((S5_COLS, tm, LANES), F32), pltpu.VMEM((tm, D_MODEL), BF16),
                        pltpu.VMEM((tm, D_MODEL), F32)],
        compiler_params=_cparams(("parallel", "arbitrary"), VMEM_LIMIT_S5_MERGE),
        name="s5_merge",
    )(h, g.reshape(1, D_MODEL), proj, proj, proj, proj, ya, yb, o1, o2, o3, s1, s2, s3, sb, sc, ar, ai, alr, ali,
      s5_d.reshape(1, S5_WIDTH), glu_w.astype(BF16), glu_b.reshape(1, S5_WIDTH), wg, wa, wb, wc, wd, wo)


def _ffn_kernel(h_ref, hp_ref, g_ref, wu_ref, cw_ref, cb_ref, wd_ref, fg_ref, o_ref, hn_sc, z_sc, *, final_norm):
    i = pl.program_id(1)
    tm, tf, halo = FFN_TM, FFN_TF, FFN_HALO

    def norm(x):
        ms = jnp.mean(x * x, axis=-1, keepdims=True)
        return x * lax.rsqrt(ms + EPS) * g_ref[...]

    def stage(t):
        hn_sc[t, halo:, :] = norm(h_ref[t * tm:(t + 1) * tm, :]).astype(BF16)
        if t == 0:
            hn_sc[t, :halo, :] = jnp.where(i == 0, 0.0, norm(hp_ref[...])).astype(BF16)
        else:
            hn_sc[t, :halo, :] = norm(h_ref[t * tm - halo:t * tm, :]).astype(BF16)

    def conv(t, cols):
        u = _dot(hn_sc[t], wu_ref[:, cols])
        u1 = pltpu.roll(u, 1, 0)
        u2 = pltpu.roll(u, 2, 0)
        return (cw_ref[0:1, cols] * u2[halo:, :] + cw_ref[1:2, cols] * u1[halo:, :] + cw_ref[2:3, cols] * u[halo:, :]
                + cb_ref[:, cols])

    stage(0)
    for t in range(FFN_SUBTILES):
        for c in range(D_FF // tf):
            a = conv(t, slice(c * tf, (c + 1) * tf))
            b = conv(t, slice(D_FF + c * tf, D_FF + (c + 1) * tf))
            z_sc[t, :, c * tf:(c + 1) * tf] = ((a * _sigmoid(a)) * b).astype(BF16)
            if c == 0 and t + 1 < FFN_SUBTILES:
                stage(t + 1)
        rows = slice(t * tm, (t + 1) * tm)
        y = h_ref[rows, :] + _dot(z_sc[t], wd_ref[...])
        if final_norm:
            ms = jnp.mean(y * y, axis=-1, keepdims=True)
            y = y * lax.rsqrt(ms + EPS) * fg_ref[...]
        o_ref[rows, :] = y


def _ffn(h, g, w_up, conv_w, conv_b, w_down, final_g, final_norm):
    bsz, seq, _ = h.shape
    tm, halo = FFN_TM, FFN_HALO
    rows = FFN_SUBTILES * tm
    return pl.pallas_call(
        functools.partial(_ffn_kernel, final_norm=final_norm),
        out_shape=jax.ShapeDtypeStruct((bsz, seq, D_MODEL), F32),
        grid=(bsz, seq // rows),
        in_specs=[pl.BlockSpec((None, rows, D_MODEL), lambda b, i: (b, i, 0)),
                  pl.BlockSpec((None, halo, D_MODEL), lambda b, i: (b, jnp.maximum(i * (rows // halo) - 1, 0), 0)),
                  _resident((1, D_MODEL)), _resident((D_MODEL, 2 * D_FF)), _resident((CONV_WIDTH, 2 * D_FF)),
                  _resident((1, 2 * D_FF)), _resident((D_FF, D_MODEL)), _resident((1, D_MODEL))],
        out_specs=pl.BlockSpec((None, rows, D_MODEL), lambda b, i: (b, i, 0)),
        scratch_shapes=[pltpu.VMEM((FFN_SUBTILES, tm + halo, D_MODEL), BF16),
                        pltpu.VMEM((FFN_SUBTILES, tm, D_FF), BF16)],
        compiler_params=_cparams(("parallel", "arbitrary")),
        name="conv_ffn",
    )(h, h, g.reshape(1, D_MODEL), w_up, conv_w, conv_b.reshape(1, -1), w_down, final_g.reshape(1, D_MODEL))


def kernel(x, norm1_g, w_in, hgrn_lb_logits, hgrn_norm_g, attn_sinks, s5_a_re, s5_a_im, s5_b_re, s5_b_im, s5_c_re, s5_c_im, s5_d, s5_log_step, s5_glu_w, s5_glu_b, w_branch_a, w_branch_b, w_branch_c, w_branch_d, w_out, norm2_g, ffn_w_up, ffn_conv_w, ffn_conv_b, ffn_w_down, final_norm_g):
    bsz, seq, _ = x.shape
    tables = _rope_tables(seq)
    p = jax.nn.softmax(hgrn_lb_logits.astype(F32), axis=0)
    lower_bounds = jnp.cumsum(p, axis=0) - p[0]

    w_in_bf16 = w_in.astype(BF16)
    s5_all = jax.vmap(_s5_params)(s5_a_re, s5_a_im, s5_b_re, s5_b_im, s5_c_re, s5_c_im, s5_log_step)
    w_a, w_b, w_c, w_d, w_o = (w.astype(BF16) for w in (w_branch_a, w_branch_b, w_branch_c, w_branch_d, w_out))
    w_up, w_down, glu_w = ffn_w_up.astype(BF16), ffn_w_down.astype(BF16), s5_glu_w.astype(BF16)

    h = x
    for l in range(DEPTH):
        proj, y_a = _proj_hgrn(h, norm1_g[l], w_in_bf16[l], tables, lower_bounds[l], hgrn_norm_g[l])
        y_b = _band_attn(proj, OFF_SWA_Q, OFF_SWA_K, OFF_SWA_V, SWA_Q_WIDTH, SWA_KV_WIDTH, dil=1, nq=SWA_NQ,
                         max_dist=SWA_WINDOW - 1, sinks=attn_sinks[l].astype(F32))
        dil = [_band_attn(proj, OFF_DIL_Q + g * DIL_OUT, OFF_DIL_K + g * DIL_OUT, OFF_DIL_V + g * DIL_OUT, DIL_OUT,
                          DIL_OUT, dil=dilation, nq=DIL_NQ[g], max_dist=window // dilation)
               for g, (window, dilation) in enumerate(DIL_PAIRS)]
        h = _s5_merge(h, norm1_g[l], proj, tuple(t[l] for t in s5_all), s5_d[l], glu_w[l], s5_glu_b[l], y_a, y_b, dil,
                      w_in_bf16[l][:, D_IN:], w_a[l], w_b[l], w_c[l], w_d[l], w_o[l])
        h = _ffn(h, norm2_g[l], w_up[l], ffn_conv_w[l], ffn_conv_b[l], w_down[l], final_norm_g,
                 final_norm=(l == DEPTH - 1))
    return h
```
